```python
import math
import jax
import jax.numpy as jnp
from jax import lax
import numpy as np

D_MODEL = 1024
BATCH = 2
SEQ = 16384
DEPTH = 4

CHUNK = 64
D_MIX = D_MODEL
GROUP_W = D_MIX // 4
NORM_EPS = 1e-6
MASK_VALUE = -1e30
GATE_FLOOR = 1e-20

A_KERNEL = 31
B_HEADS = 4
B_DK = GROUP_W // B_HEADS
B_DV = GROUP_W // B_HEADS
C_HEADS = 4
C_HEADDIM = GROUP_W // C_HEADS
C_GROUPS = 2
C_STATE = 128
C_CONV = 4
C_XBC = GROUP_W + 2 * C_GROUPS * C_STATE
D_HEADS = 4
D_Q_RANK = 256
D_KV_RANK = 128
D_NOPE = 64
D_ROPE = 32
D_VDIM = GROUP_W // D_HEADS
ROPE_THETA = 10000.0
Q_BLOCK = 128

IN_SPLITS = (GROUP_W, GROUP_W, GROUP_W,
             GROUP_W, GROUP_W, GROUP_W, GROUP_W,
             C_XBC, C_HEADS, GROUP_W,
             D_Q_RANK, D_KV_RANK, D_ROPE, GROUP_W)
N_IN = sum(IN_SPLITS)

kernel_name = 'hymba_style_streaming_hybrid_trunk'


def rms_norm(x, g, eps=NORM_EPS):
    xf = x.astype(jnp.float32)
    y = xf * lax.rsqrt(jnp.mean(jnp.square(xf), axis=-1, keepdims=True) + eps)
    return (y * g.astype(jnp.float32)).astype(x.dtype)


def layer_norm(x, g, b, eps=1e-5):
    xf = x.astype(jnp.float32)
    mu = jnp.mean(xf, axis=-1, keepdims=True)
    var = jnp.mean(jnp.square(xf - mu), axis=-1, keepdims=True)
    y = (xf - mu) * lax.rsqrt(var + eps)
    return (y * g.astype(jnp.float32) + b.astype(jnp.float32)).astype(x.dtype)


def masked_exp(mask, logd):
    return jnp.where(mask, jnp.exp(jnp.where(mask, logd, 0.0)), 0.0)


def causal_depthwise_conv(x, w, b):
    k, c = w.shape
    y = lax.conv_general_dilated(
        x, w[:, None, :].astype(x.dtype), window_strides=(1,), padding=[(k - 1, 0)],
        dimension_numbers=('NWC', 'WIO', 'NWC'), feature_group_count=c)
    return y + b.astype(x.dtype)


def apply_rotary(x, positions):
    r = x.shape[-1]
    inv_freq = ROPE_THETA ** (-jnp.arange(0, r, 2, dtype=jnp.float32) / r)
    ang = positions.astype(jnp.float32)[..., None] * inv_freq
    if x.ndim == 4:
        ang = ang[:, :, None, :]
    cos, sin = jnp.cos(ang), jnp.sin(ang)
    xf = x.astype(jnp.float32)
    x1, x2 = xf[..., : r // 2], xf[..., r // 2:]
    out = jnp.concatenate([x1 * cos - x2 * sin, x2 * cos + x1 * sin], axis=-1)
    return out.astype(x.dtype)


def conformer_conv_branch(val, glu_gate, gate, dw_w, dw_b, ln_g, ln_b, pw_w, pw_b):
    h = val * jax.nn.sigmoid(glu_gate)
    h = causal_depthwise_conv(h, dw_w, dw_b)
    h = jax.nn.silu(layer_norm(h, ln_g, ln_b))
    h = h @ pw_w + pw_b
    return h * jax.nn.silu(gate)


def hgrn2_branch(q, f_raw, v, gate, lb, norm_g):
    bsz, s, _ = q.shape
    nc = s // CHUNK
    f32 = jnp.float32
    zf = f_raw.astype(f32)
    lbf = lb.astype(f32)
    fgate = lbf + (1.0 - lbf) * jax.nn.sigmoid(zf)
    log_f = jnp.log(jnp.maximum(fgate, GATE_FLOOR))
    k = (1.0 - lbf) * jax.nn.sigmoid(-zf)

    def to_chunks(t, d):
        return t.astype(f32).reshape(bsz, nc, CHUNK, B_HEADS, d).transpose(1, 0, 3, 2, 4)

    qc, kc, gc = to_chunks(q, B_DK), to_chunks(k, B_DK), to_chunks(log_f, B_DK)
    vc = to_chunks(v, B_DV)
    causal = jnp.tril(jnp.ones((CHUNK, CHUNK), dtype=bool))[:, :, None]

    def step(state, inp):
        qt, kt, vt, gt = inp
        b = jnp.cumsum(gt, axis=2)
        o_inter = jnp.einsum('bhtk,bhkv->bhtv', qt * jnp.exp(b), state)
        diff = b[:, :, :, None, :] - b[:, :, None, :, :]
        decay = masked_exp(causal, diff)
        scores = jnp.einsum('bhtk,bhsk,bhtsk->bhts', qt, kt, decay)
        o_intra = jnp.einsum('bhts,bhsv->bhtv', scores, vt)
        b_last = b[:, :, -1:, :]
        new_state = (jnp.exp(b_last[:, :, 0, :])[..., None] * state
                     + jnp.einsum('bhsk,bhsv->bhkv', kt * jnp.exp(b_last - b), vt))
        return new_state, o_inter + o_intra

    state0 = jnp.zeros((bsz, B_HEADS, B_DK, B_DV), f32)
    _, o = lax.scan(step, state0, (qc, kc, vc, gc))
    o = o.transpose(1, 0, 3, 2, 4).reshape(bsz, s, B_HEADS, B_DV)
    o = rms_norm(o, norm_g.reshape(B_HEADS, B_DV)).astype(q.dtype)
    return o.reshape(bsz, s, B_HEADS * B_DV) * jax.nn.silu(gate)


def ssd_branch(xbc, dt_raw, z, conv_w, conv_b, dt_bias, a_log, d_skip, norm_g):
    bsz, s, _ = xbc.shape
    nc = s // CHUNK
    e = C_HEADS // C_GROUPS
    f32 = jnp.float32
    xbc = jax.nn.silu(causal_depthwise_conv(xbc, conv_w, conv_b))
    xs, bm, cm = jnp.split(xbc, [GROUP_W, GROUP_W + C_GROUPS * C_STATE], axis=-1)
    xs = xs.astype(f32).reshape(bsz, nc, CHUNK, C_GROUPS, e, C_HEADDIM)
    bm = bm.astype(f32).reshape(bsz, nc, CHUNK, C_GROUPS, C_STATE)
    cm = cm.astype(f32).reshape(bsz, nc, CHUNK, C_GROUPS, C_STATE)
    dt = jax.nn.softplus(dt_raw.astype(f32) + dt_bias.astype(f32))
    a = (dt * (-jnp.exp(a_log.astype(f32)))).reshape(bsz, nc, CHUNK, C_GROUPS, e)
    dt = dt.reshape(bsz, nc, CHUNK, C_GROUPS, e)
    a_cs = jnp.cumsum(a, axis=2)
    xdt = xs * dt[..., None]
    causal = jnp.tril(jnp.ones((CHUNK, CHUNK), dtype=bool))[:, :, None, None]
    seg = a_cs[:, :, :, None] - a_cs[:, :, None, :]
    lmat = masked_exp(causal, seg)
    cb = jnp.einsum('bclgn,bcsgn->bclsg', cm, bm)
    scores = cb[..., None] * lmat
    y_diag = jnp.einsum('bclsge,bcsgep->bclgep', scores, xdt)
    wx = xdt * jnp.exp(a_cs[:, :, -1:] - a_cs)[..., None]
    states = jnp.einsum('bclgn,bclgep->bcgepn', bm, wx)
    chunk_decay = jnp.exp(a_cs[:, :, -1])

    def step(h, inp):
        st, dc = inp
        return dc[..., None, None] * h + st, h

    h0 = jnp.zeros((bsz, C_GROUPS, e, C_HEADDIM, C_STATE), f32)
    _, prev = lax.scan(step, h0, (states.transpose(1, 0, 2, 3, 4, 5),
                                  chunk_decay.transpose(1, 0, 2, 3)))
    prev = prev.transpose(1, 0, 2, 3, 4, 5)
    y_off = jnp.einsum('bclgn,bcgepn->bclgep', cm, prev) * jnp.exp(a_cs)[..., None]
    y = y_diag + y_off + xs * d_skip.astype(f32).reshape(C_GROUPS, e)[..., None]
    y = y.reshape(bsz, s, C_GROUPS, e * C_HEADDIM)
    zg = jax.nn.silu(z.astype(f32)).reshape(bsz, s, C_GROUPS, e * C_HEADDIM)
    y = rms_norm(y * zg, norm_g.reshape(C_GROUPS, e * C_HEADDIM))
    return y.reshape(bsz, s, GROUP_W).astype(z.dtype)


def mla_branch(cq, ckv, k_rope, gate, positions, qa_g, qb_w, kva_g, kvb_w):
    bsz, s, _ = cq.shape
    q = (rms_norm(cq, qa_g) @ qb_w).reshape(bsz, s, D_HEADS, D_NOPE + D_ROPE)
    q = jnp.concatenate([q[..., :D_NOPE], apply_rotary(q[..., D_NOPE:], positions)], axis=-1)
    kv = (rms_norm(ckv, kva_g) @ kvb_w).reshape(bsz, s, D_HEADS, D_NOPE + D_VDIM)
    k_nope, v = kv[..., :D_NOPE], kv[..., D_NOPE:]
    k_r = apply_rotary(k_rope, positions)
    k = jnp.concatenate(
        [k_nope, jnp.broadcast_to(k_r[:, :, None, :], (bsz, s, D_HEADS, D_ROPE))], axis=-1)
    scale = (D_NOPE + D_ROPE) ** -0.5
    nqb = s // Q_BLOCK
    key_chunk = jnp.arange(s) // CHUNK
    q_blocks = q.reshape(bsz, nqb, Q_BLOCK, D_HEADS, D_NOPE + D_ROPE).transpose(1, 0, 2, 3, 4)
    q_chunk = key_chunk.reshape(nqb, Q_BLOCK)

    def attend(inp):
        qb, qch = inp
        sc = jnp.einsum('bqhd,bkhd->bhqk', qb, k).astype(jnp.float32) * scale
        mask = key_chunk[None, :] <= qch[:, None]
        p = jax.nn.softmax(jnp.where(mask, sc, MASK_VALUE), axis=-1).astype(v.dtype)
        return jnp.einsum('bhqk,bkhd->bqhd', p, v)

    o = lax.map(attend, (q_blocks, q_chunk))
    o = o.transpose(1, 0, 2, 3, 4).reshape(bsz, s, D_HEADS * D_VDIM)
    return o * jax.nn.silu(gate)


def setup_inputs(seed: int = 0) -> dict:
    key = jax.random.key(seed)
    ks = jax.random.split(key, 24)
    f32 = jnp.float32

    def nrm(k, shape, scale):
        return jax.random.normal(k, shape, f32) * scale

    def gain(k, shape):
        return 1.0 + 0.02 * jax.random.normal(k, shape, f32)

    x = nrm(ks[0], (BATCH, SEQ, D_MODEL), 1.0)
    offsets = jax.random.randint(ks[1], (BATCH, 1), 0, 4096, dtype=jnp.int32)
    positions = offsets + jnp.arange(SEQ, dtype=jnp.int32)[None, :]
    dt0 = jnp.exp(jax.random.uniform(ks[16], (DEPTH, C_HEADS), f32,
                                     math.log(1e-3), math.log(1e-1)))
    return {
        'x': x,
        'positions': positions,
        'pre_norm_g': gain(ks[2], (DEPTH, D_MODEL)),
        'post_norm_g': gain(ks[3], (DEPTH, D_MODEL)),
        'w_in': nrm(ks[4], (DEPTH, D_MODEL, N_IN), D_MODEL ** -0.5),
        'w_out': nrm(ks[5], (DEPTH, D_MIX, D_MODEL), D_MIX ** -0.5),
        'a_dw_w': nrm(ks[6], (DEPTH, A_KERNEL, GROUP_W), A_KERNEL ** -0.5),
        'a_dw_b': nrm(ks[7], (DEPTH, GROUP_W), 0.02),
        'a_ln_g': gain(ks[8], (DEPTH, GROUP_W)),
        'a_ln_b': nrm(ks[9], (DEPTH, GROUP_W), 0.02),
        'a_pw_w': nrm(ks[10], (DEPTH, GROUP_W, GROUP_W), GROUP_W ** -0.5),
        'a_pw_b': nrm(ks[11], (DEPTH, GROUP_W), 0.02),
        'b_lb_logits': 1.0 + nrm(ks[12], (DEPTH, B_HEADS * B_DK), 0.1),
        'b_norm_g': gain(ks[13], (DEPTH, B_HEADS * B_DV)),
        'c_conv_w': nrm(ks[14], (DEPTH, C_CONV, C_XBC), C_CONV ** -0.5),
        'c_conv_b': nrm(ks[15], (DEPTH, C_XBC), 0.02),
        'c_dt_bias': dt0 + jnp.log(-jnp.expm1(-dt0)),
        'c_a_log': jnp.log(jax.random.uniform(ks[17], (DEPTH, C_HEADS), f32, 1.0, 16.0)),
        'c_d': 1.0 + nrm(ks[18], (DEPTH, C_HEADS), 0.1),
        'c_norm_g': gain(ks[19], (DEPTH, GROUP_W)),
        'd_qa_g': gain(ks[20], (DEPTH, D_Q_RANK)),
        'd_qb_w': nrm(ks[21], (DEPTH, D_Q_RANK, D_HEADS * (D_NOPE + D_ROPE)), D_Q_RANK ** -0.5),
        'd_kva_g': gain(ks[22], (DEPTH, D_KV_RANK)),
        'd_kvb_w': nrm(ks[23], (DEPTH, D_KV_RANK, D_HEADS * (D_NOPE + D_VDIM)), D_KV_RANK ** -0.5),
    }


def reference(x, positions, pre_norm_g, post_norm_g, w_in, w_out,
              a_dw_w, a_dw_b, a_ln_g, a_ln_b, a_pw_w, a_pw_b,
              b_lb_logits, b_norm_g,
              c_conv_w, c_conv_b, c_dt_bias, c_a_log, c_d, c_norm_g,
              d_qa_g, d_qb_w, d_kva_g, d_kvb_w):
    lb_p = jax.nn.softmax(b_lb_logits.astype(jnp.float32), axis=0)
    lower_bounds = jnp.cumsum(lb_p, axis=0) - lb_p[0:1]
    split_at = [int(v) for v in np.cumsum(IN_SPLITS)[:-1]]
    for l in range(DEPTH):
        h = rms_norm(x, pre_norm_g[l])
        proj = h @ w_in[l]
        (a_val, a_glu, a_gate, b_q, b_f, b_i, b_gate,
         c_xbc, c_dt, c_z, d_cq, d_ckv, d_kr, d_gate) = jnp.split(proj, split_at, axis=-1)
        y_a = conformer_conv_branch(a_val, a_glu, a_gate, a_dw_w[l], a_dw_b[l],
                                    a_ln_g[l], a_ln_b[l], a_pw_w[l], a_pw_b[l])
        y_b = hgrn2_branch(b_q, b_f, b_i, b_gate, lower_bounds[l], b_norm_g[l])
        y_c = ssd_branch(c_xbc, c_dt, c_z, c_conv_w[l], c_conv_b[l], c_dt_bias[l],
                         c_a_log[l], c_d[l], c_norm_g[l])
        y_d = mla_branch(d_cq, d_ckv, d_kr, d_gate, positions, d_qa_g[l], d_qb_w[l],
                         d_kva_g[l], d_kvb_w[l])
        y = jnp.concatenate([y_a, y_b, y_c, y_d], axis=-1) @ w_out[l]
        x = x + rms_norm(y, post_norm_g[l])
    return x
```

```python
import functools
import math

import jax
import jax.numpy as jnp
from jax import lax
from jax.experimental import pallas as pl
from jax.experimental.pallas import tpu as pltpu

F32 = jnp.float32
BF16 = jnp.bfloat16

D_MODEL = 1024
CHUNK = 64
GROUP_W = 256
NORM_EPS = 1e-6
LN_EPS = 1e-5
MASK_VALUE = -1e30
GATE_FLOOR = 1e-20
A_KERNEL = 31
B_HEADS = 4
B_DK = 64
C_HEADS = 4
C_HEADDIM = 64
C_GROUPS = 2
C_STATE = 128
C_CONV = 4
C_XBC = 768
D_HEADS = 4
D_Q_RANK = 256
D_KV_RANK = 128
D_NOPE = 64
D_ROPE = 32
D_VDIM = 64
ROPE_THETA = 10000.0

LANES = 128
SUB = 16
VMEM_LIMIT = 48 * 1024 * 1024

W_PA, W_PB, W_PC, W_PD, W_PM = 768, 1024, 1024, 640, 256
N_PROJ = W_PA + W_PB + W_PC + W_PD + W_PM

NT = (((1,), (1,)), ((), ()))
TN = (((0,), (0,)), ((), ()))


def _params(*sem):
    return pltpu.CompilerParams(dimension_semantics=sem, vmem_limit_bytes=VMEM_LIMIT)


def _layer_spec(arr, l):
    shape = arr.shape[1:]
    nd = len(shape)
    return pl.BlockSpec((None,) + shape, lambda *_: (l,) + (0,) * nd)


def _split3(x):
    h1 = x.astype(BF16)
    r1 = x - h1.astype(F32)
    h2 = r1.astype(BF16)
    h3 = (r1 - h2.astype(F32)).astype(BF16)
    return h1, h2, h3


def _dot_sel(sel, x, dims=None, sel_on_right=False):
    if dims is None:
        dims = (((1,), (0,)), ((), ()))
    out = None
    for piece in _split3(x):
        a, b = (piece, sel) if sel_on_right else (sel, piece)
        t = lax.dot_general(a, b, dims, preferred_element_type=F32)
        out = t if out is None else out + t
    return out


def _silu(x):
    return x * jax.nn.sigmoid(x)


def _rms(x, g):
    ms = jnp.mean(x * x, axis=-1, keepdims=True)
    return x * lax.rsqrt(ms + NORM_EPS) * g


def _inproj_body(x_ref, g_ref, w_ref, *outs):
    h = _rms(x_ref[...], g_ref[...]).astype(BF16)
    off = 0
    for o in outs:
        n = o.shape[-1]
        o[...] = jnp.dot(h, w_ref[:, off:off + n], preferred_element_type=F32).astype(o.dtype)
        off += n


def _inproj(x, pre_g, w1, l, tm):
    t = x.shape[0]
    widths = (W_PA, W_PB, W_PC, W_PD, W_PM)
    return pl.pallas_call(
        _inproj_body,
        grid=(t // tm,),
        in_specs=[pl.BlockSpec((tm, D_MODEL), lambda i: (i, 0)),
                  _layer_spec(pre_g, l), _layer_spec(w1, l)],
        out_specs=[pl.BlockSpec((tm, n), lambda i: (i, 0)) for n in widths],
        out_shape=[jax.ShapeDtypeStruct((t, n), F32) for n in widths],
        compiler_params=_params("parallel"),
        name="inproj",
    )(x, pre_g, w1)


A_HALO = 32


def _a_body(pa_ref, ph_ref, dww_ref, dwb_ref, lng_ref, lnb_ref, pww_ref, pwb_ref, o_ref,
            hscr, cscr, *, ts, nseq):
    first = (pl.program_id(0) % nseq) == 0
    h = pa_ref[:, 0:256] * jax.nn.sigmoid(pa_ref[:, 256:512])
    hh = ph_ref[:, 0:256] * jax.nn.sigmoid(ph_ref[:, 256:512])
    hscr[0:A_HALO, :] = jnp.where(first, 0.0, hh)
    hscr[A_HALO:A_HALO + ts, :] = h
    w = dww_ref[...]
    rows = 64
    base = A_HALO - (A_KERNEL - 1)
    for r in range(ts // rows):
        acc = jnp.broadcast_to(dwb_ref[...], (rows, GROUP_W))
        for j in range(A_KERNEL):
            s0 = r * rows + base + j
            acc = acc + w[j:j + 1, :] * hscr[s0:s0 + rows, :]
        cscr[r * rows:(r + 1) * rows, :] = acc
    c = cscr[...]
    mu = jnp.mean(c, axis=-1, keepdims=True)
    d = c - mu
    var = jnp.mean(d * d, axis=-1, keepdims=True)
    y = d * lax.rsqrt(var + LN_EPS) * lng_ref[...] + lnb_ref[...]
    y = _silu(y).astype(BF16)
    y = jnp.dot(y, pww_ref[...], preferred_element_type=F32) + pwb_ref[...]
    o_ref[...] = y * _silu(pa_ref[:, 512:768])


def _branch_a(pa, dww, dwb, lng, lnb, pww, pwb, l, seq, ts):
    t = pa.shape[0]
    nseq = seq // ts
    hb = ts // A_HALO
    return pl.pallas_call(
        functools.partial(_a_body, ts=ts, nseq=nseq),
        grid=(t // ts,),
        in_specs=[pl.BlockSpec((ts, W_PA), lambda i: (i, 0)),
                  pl.BlockSpec((A_HALO, W_PA), lambda i: (jnp.maximum(i * hb - 1, 0), 0)),
                  _layer_spec(dww, l), _layer_spec(dwb, l), _layer_spec(lng, l),
                  _layer_spec(lnb, l), _layer_spec(pww, l), _layer_spec(pwb, l)],
        out_specs=pl.BlockSpec((ts, GROUP_W), lambda i: (i, 0)),
        out_shape=jax.ShapeDtypeStruct((t, GROUP_W), F32),
        scratch_shapes=[pltpu.VMEM((A_HALO + ts, GROUP_W), F32),
                        pltpu.VMEM((ts, GROUP_W), F32)],
        compiler_params=_params("arbitrary"),
        name="branch_a",
    )(pa, pa, dww, dwb, lng, lnb, pww, pwb)


def _b_body(pb_ref, lbl_ref, ng_ref, o_ref, st_ref, kscr, bscr, oscr, wscr, *, l, tb):
    @pl.when(pl.program_id(1) == 0)
    def _():
        st_ref[...] = jnp.zeros_like(st_ref)

    lg = lbl_ref[...]
    e = jnp.exp(lg - jnp.max(lg, axis=0, keepdims=True))
    p = e / jnp.sum(e, axis=0, keepdims=True)
    lb = jnp.zeros((1, GROUP_W), F32)
    for m in range(1, l + 1):
        lb = lb + p[m:m + 1, :]

    zf = pb_ref[:, 256:512]
    fg = lb + (1.0 - lb) * jax.nn.sigmoid(zf)
    logf = jnp.log(jnp.maximum(fg, GATE_FLOOR))
    kscr[...] = (1.0 - lb) * jax.nn.sigmoid(-zf)
    ri = lax.broadcasted_iota(jnp.int32, (tb, tb), 0)
    ci = lax.broadcasted_iota(jnp.int32, (tb, tb), 1)
    tril = jnp.where((ri // SUB == ci // SUB) & (ci <= ri), 1.0, 0.0).astype(BF16)
    bscr[...] = _dot_sel(tril, logf)

    hr = lax.broadcasted_iota(jnp.int32, (GROUP_W, GROUP_W), 0) // B_DK
    hc = lax.broadcasted_iota(jnp.int32, (GROUP_W, GROUP_W), 1) // B_DK
    same_head = hr == hc
    bd = jnp.where(same_head, 1.0, 0.0).astype(BF16)
    tio = lax.broadcasted_iota(jnp.int32, (SUB, GROUP_W), 0)

    def sub(jj, carry):
        r0 = pl.multiple_of(jj * SUB, SUB)
        qj = pb_ref[pl.ds(r0, SUB), 0:256]
        vj = pb_ref[pl.ds(r0, SUB), 512:768]
        kj = kscr[pl.ds(r0, SUB), :]
        bj = bscr[pl.ds(r0, SUB), :]
        for s in range(SUB):
            dec = jnp.exp(jnp.minimum(bj - bj[s:s + 1, :], 0.0))
            w = jnp.where(tio >= s, qj * kj[s:s + 1, :] * dec, 0.0)
            wscr[s * SUB:(s + 1) * SUB, :] = w.astype(BF16)
        m = jnp.dot(wscr[...], bd, preferred_element_type=F32)
        o = jnp.zeros((SUB, GROUP_W), F32)
        for s in range(SUB):
            o = o + m[s * SUB:(s + 1) * SUB, :] * vj[s:s + 1, :]
        st = st_ref[...]
        qt = (qj * jnp.exp(bj)).astype(BF16)
        o = o + lax.dot_general(qt, st.astype(BF16), NT, preferred_element_type=F32)
        oscr[pl.ds(r0, SUB), :] = o
        blast = bj[SUB - 1:SUB, :]
        kh = (kj * jnp.exp(blast - bj)).astype(BF16)
        u = lax.dot_general(vj.astype(BF16), kh, TN, preferred_element_type=F32)
        st_ref[...] = st * jnp.exp(blast) + jnp.where(same_head, u, 0.0)
        return carry

    lax.fori_loop(0, tb // SUB, sub, 0)

    o = oscr[...]
    ms = _dot_sel(bd, o * o, sel_on_right=True) * (1.0 / B_DK)
    y = o * lax.rsqrt(ms + NORM_EPS) * ng_ref[...]
    o_ref[...] = y * _silu(pb_ref[:, 768:1024])


def _branch_b(pb, lbl, ng, l, batch, seq, tb):
    t = pb.shape[0]
    nseq = seq // tb
    return pl.pallas_call(
        functools.partial(_b_body, l=l, tb=tb),
        grid=(batch, nseq),
        in_specs=[pl.BlockSpec((tb, W_PB), lambda b, j: (b * nseq + j, 0)),
                  pl.BlockSpec(lbl.shape, lambda b, j: (0, 0)),
                  _layer_spec(ng, l)],
        out_specs=pl.BlockSpec((tb, GROUP_W), lambda b, j: (b * nseq + j, 0)),
        out_shape=jax.ShapeDtypeStruct((t, GROUP_W), F32),
        scratch_shapes=[pltpu.VMEM((GROUP_W, GROUP_W), F32),
                        pltpu.VMEM((tb, GROUP_W), F32),
                        pltpu.VMEM((tb, GROUP_W), F32),
                        pltpu.VMEM((tb, GROUP_W), F32),
                        pltpu.VMEM((SUB * SUB, GROUP_W), BF16)],
        compiler_params=_params("arbitrary", "arbitrary"),
        name="branch_b",
    )(pb, lbl, ng)


C_HALO = 8


def _c_body(pc_ref, ph_ref, pm_ref, cw_ref, cb_ref, dtb_ref, alog_ref, dsk_ref, ng_ref, o_ref,
            h_ref, xscr, cscr, ascr, dscr, yscr, *, tc):
    first = pl.program_id(1) == 0

    @pl.when(first)
    def _():
        h_ref[...] = jnp.zeros_like(h_ref)

    xscr[0:C_HALO, :] = jnp.where(first, 0.0, ph_ref[:, 0:C_XBC])
    xscr[C_HALO:C_HALO + tc, :] = pc_ref[:, 0:C_XBC]
    w = cw_ref[...]
    rows = 64
    base = C_HALO - (C_CONV - 1)
    for r in range(tc // rows):
        acc = jnp.broadcast_to(cb_ref[...], (rows, C_XBC))
        for j in range(C_CONV):
            s0 = r * rows + base + j
            acc = acc + w[j:j + 1, :] * xscr[s0:s0 + rows, :]
        cscr[r * rows:(r + 1) * rows, :] = _silu(acc)

    dt = jnp.logaddexp(pm_ref[:, 0:LANES] + dtb_ref[...], 0.0)
    dscr[...] = dt
    a = dt * (-jnp.exp(alog_ref[...]))
    ri = lax.broadcasted_iota(jnp.int32, (tc, tc), 0)
    ci = lax.broadcasted_iota(jnp.int32, (tc, tc), 1)
    tril = jnp.where((ri // CHUNK == ci // CHUNK) & (ci <= ri), 1.0, 0.0).astype(BF16)
    ascr[...] = _dot_sel(tril, a)

    sel = jnp.where(lax.broadcasted_iota(jnp.int32, (8, LANES), 0)
                    == lax.broadcasted_iota(jnp.int32, (8, LANES), 1), 1.0, 0.0).astype(BF16)
    li = lax.broadcasted_iota(jnp.int32, (CHUNK, CHUNK), 0)
    si = lax.broadcasted_iota(jnp.int32, (CHUNK, CHUNK), 1)
    causal = li >= si
    dsk = dsk_ref[...]
    e = C_HEADS // C_GROUPS

    def chunk(c, carry):
        r0 = pl.multiple_of(c * CHUNK, CHUNK)
        acs = ascr[pl.ds(r0, CHUNK), :]
        dtc = dscr[pl.ds(r0, CHUNK), :]
        arow = _dot_sel(sel, acs, NT)
        xs = cscr[pl.ds(r0, CHUNK), 0:GROUP_W]
        for g in range(C_GROUPS):
            bmg = cscr[pl.ds(r0, CHUNK), GROUP_W + g * C_STATE:GROUP_W + (g + 1) * C_STATE].astype(BF16)
            c0 = GROUP_W + C_GROUPS * C_STATE + g * C_STATE
            cmg = cscr[pl.ds(r0, CHUNK), c0:c0 + C_STATE].astype(BF16)
            cb = lax.dot_general(cmg, bmg, NT, preferred_element_type=F32)
            for ee in range(e):
                h = g * e + ee
                col = acs[:, h:h + 1]
                row = arow[h:h + 1, :]
                lm = jnp.where(causal, jnp.exp(jnp.minimum(col - row, 0.0)), 0.0)
                xsh = xs[:, h * C_HEADDIM:(h + 1) * C_HEADDIM]
                xdt = xsh * dtc[:, h:h + 1]
                ydiag = jnp.dot((cb * lm).astype(BF16), xdt.astype(BF16), preferred_element_type=F32)
                hprev = h_ref[h * C_HEADDIM:(h + 1) * C_HEADDIM, :]
                yoff = lax.dot_general(cmg, hprev.astype(BF16), NT, preferred_element_type=F32)
                yoff = yoff * jnp.exp(col)
                alast = row[:, CHUNK - 1:CHUNK]
                wx = (xdt * jnp.exp(alast - col)).astype(BF16)
                stt = lax.dot_general(wx, bmg, TN, preferred_element_type=F32)
                h_ref[h * C_HEADDIM:(h + 1) * C_HEADDIM, :] = jnp.exp(alast) * hprev + stt
                yscr[pl.ds(r0, CHUNK), h * C_HEADDIM:(h + 1) * C_HEADDIM] = (
                    ydiag + yoff + xsh * dsk[:, h * C_HEADDIM:(h + 1) * C_HEADDIM])
        return carry

    lax.fori_loop(0, tc // CHUNK, chunk, 0)

    yz = yscr[...] * _silu(pc_ref[:, C_XBC:C_XBC + GROUP_W])
    gw = e * C_HEADDIM
    for g in range(C_GROUPS):
        o_ref[:, g * gw:(g + 1) * gw] = _rms(yz[:, g * gw:(g + 1) * gw], ng_ref[:, g * gw:(g + 1) * gw])


def _branch_c(pc, pm, cw, cb, dtb, alog, dsk, ng, l, batch, seq, tc):
    t = pc.shape[0]
    nseq = seq // tc
    hb = tc // C_HALO
    return pl.pallas_call(
        functools.partial(_c_body, tc=tc),
        grid=(batch, nseq),
        in_specs=[pl.BlockSpec((tc, W_PC), lambda b, j: (b * nseq + j, 0)),
                  pl.BlockSpec((C_HALO, W_PC), lambda b, j: (jnp.maximum((b * nseq + j) * hb - 1, 0), 0)),
                  pl.BlockSpec((tc, W_PM), lambda b, j: (b * nseq + j, 0)),
                  _layer_spec(cw, l), _layer_spec(cb, l), _layer_spec(dtb, l),
                  _layer_spec(alog, l), _layer_spec(dsk, l), _layer_spec(ng, l)],
        out_specs=pl.BlockSpec((tc, GROUP_W), lambda b, j: (b * nseq + j, 0)),
        out_shape=jax.ShapeDtypeStruct((t, GROUP_W), F32),
        scratch_shapes=[pltpu.VMEM((C_HEADS * C_HEADDIM, C_STATE), F32),
                        pltpu.VMEM((C_HALO + tc, C_XBC), F32),
                        pltpu.VMEM((tc, C_XBC), F32),
                        pltpu.VMEM((tc, LANES), F32),
                        pltpu.VMEM((tc, LANES), F32),
                        pltpu.VMEM((tc, GROUP_W), F32)],
        compiler_params=_params("arbitrary", "arbitrary"),
        name="branch_c",
    )(pc, pc, pm, cw, cb, dtb, alog, dsk, ng)


HEAD_W = 128
QK_SCALE = (D_NOPE + D_ROPE) ** -0.5
LOG2E = 1.4426950408889634


def _d_prep_body(pd_ref, pm_ref, pos_ref, qag_ref, wq_ref, wqr_ref, kvag_ref, wk_ref, wv_ref, fr_ref,
                 q_ref, k_ref, v_ref):
    hq = _rms(pd_ref[:, 256:512], qag_ref[...]).astype(BF16)
    q0 = jnp.dot(hq, wq_ref[...], preferred_element_type=F32)
    q1 = jnp.dot(hq, wqr_ref[...], preferred_element_type=F32)
    ang = pos_ref[...] * fr_ref[...]
    cs = jnp.cos(ang)
    sn = jnp.sin(ang)
    cs4 = jnp.concatenate([cs] * D_HEADS, axis=1)
    sn4 = jnp.concatenate([sn] * D_HEADS, axis=1)
    q_ref[...] = ((q0 * cs4 + q1 * sn4) * (QK_SCALE * LOG2E)).astype(BF16)

    hkv = _rms(pd_ref[:, 512:640], kvag_ref[...]).astype(BF16)
    k0 = jnp.dot(hkv, wk_ref[...], preferred_element_type=F32)
    vv = jnp.dot(hkv, wv_ref[...], preferred_element_type=F32)
    lane = lax.broadcasted_iota(jnp.int32, ang.shape, 1)
    rope = (lane >= D_NOPE) & (lane < D_NOPE + D_ROPE)
    kr = jnp.where(rope, pm_ref[:, 0:LANES] * cs, 0.0) + pm_ref[:, LANES:2 * LANES] * sn
    k_ref[...] = (k0 + jnp.concatenate([kr] * D_HEADS, axis=1)).astype(BF16)
    one = jnp.where(lane == D_VDIM, 1.0, 0.0)
    v_ref[...] = (vv + jnp.concatenate([one] * D_HEADS, axis=1)).astype(BF16)


def _d_prep(pd, pm, pos, qag, wq, wqr, kvag, wk, wv, fr, l, tm):
    t = pd.shape[0]
    hw = D_HEADS * HEAD_W
    return pl.pallas_call(
        _d_prep_body,
        grid=(t // tm,),
        in_specs=[pl.BlockSpec((tm, W_PD), lambda i: (i, 0)),
                  pl.BlockSpec((tm, W_PM), lambda i: (i, 0)),
                  pl.BlockSpec((tm, 1), lambda i: (i, 0)),
                  _layer_spec(qag, l), _layer_spec(wq, l), _layer_spec(wqr, l),
                  _layer_spec(kvag, l), _layer_spec(wk, l), _layer_spec(wv, l),
                  pl.BlockSpec(fr.shape, lambda i: (0, 0))],
        out_specs=[pl.BlockSpec((tm, hw), lambda i: (i, 0))] * 3,
        out_shape=[jax.ShapeDtypeStruct((t, hw), BF16)] * 3,
        compiler_params=_params("parallel"),
        name="mla_prep",
    )(pd, pm, pos, qag, wq, wqr, kvag, wk, wv, fr)


ATT_HEADS_PER_STEP = 2


def _attn_body(q_ref, k_ref, v_ref, o_ref, acc, mrow, *, tq):
    i = pl.program_id(2)
    ri = lax.broadcasted_iota(jnp.int32, (tq, tq), 0) // CHUNK
    ci = lax.broadcasted_iota(jnp.int32, (tq, tq), 1) // CHUNK
    visible = ci <= ri
    for hh in range(ATT_HEADS_PER_STEP):
        lanes = slice(hh * HEAD_W, (hh + 1) * HEAD_W)
        q = q_ref[:, lanes]
        acc[...] = jnp.zeros_like(acc)
        mrow[...] = jnp.full_like(mrow, MASK_VALUE)

        def step(j, diagonal):
            k0 = pl.multiple_of(j * tq, tq)
            kt = k_ref[pl.ds(k0, tq), lanes]
            vt = v_ref[pl.ds(k0, tq), lanes]
            s = lax.dot_general(q, kt, NT, preferred_element_type=F32)
            if diagonal:
                s = jnp.where(visible, s, MASK_VALUE)
            m_old = mrow[...]
            m_new = jnp.maximum(m_old, jnp.max(s, axis=-1, keepdims=True))
            p = jnp.exp2(s - m_new).astype(BF16)
            acc[...] = jnp.exp2(m_old - m_new) * acc[...] + jnp.dot(p, vt, preferred_element_type=F32)
            mrow[...] = m_new

        def full_step(j, carry):
            step(j, False)
            return carry

        lax.fori_loop(0, i, full_step, 0)
        step(i, True)
        a = acc[...]
        o_ref[:, hh * D_VDIM:(hh + 1) * D_VDIM] = a[:, 0:D_VDIM] / a[:, D_VDIM:D_VDIM + 1]


def _attention(q, k, v, batch, seq, tq):
    t = q.shape[0]
    nq = seq // tq
    hp = D_HEADS // ATT_HEADS_PER_STEP
    wq = ATT_HEADS_PER_STEP * HEAD_W
    wo = ATT_HEADS_PER_STEP * D_VDIM
    return pl.pallas_call(
        functools.partial(_attn_body, tq=tq),
        grid=(batch, hp, nq),
        in_specs=[pl.BlockSpec((tq, wq), lambda b, h, i: (b * nq + i, h)),
                  pl.BlockSpec((seq, wq), lambda b, h, i: (b, h)),
                  pl.BlockSpec((seq, wq), lambda b, h, i: (b, h))],
        out_specs=pl.BlockSpec((tq, wo), lambda b, h, i: (b * nq + i, h)),
        out_shape=jax.ShapeDtypeStruct((t, D_HEADS * D_VDIM), F32),
        scratch_shapes=[pltpu.VMEM((tq, HEAD_W), F32), pltpu.VMEM((tq, 1), F32)],
        compiler_params=_params("arbitrary", "arbitrary", "arbitrary"),
        name="mla_attention",
    )(q, k, v)


def _out_body(ya_ref, yb_ref, yc_ref, od_ref, gd_ref, x_ref, w_ref, pg_ref, o_ref):
    yd = od_ref[...] * _silu(gd_ref[...])
    y = None
    for n, part in enumerate((ya_ref[...], yb_ref[...], yc_ref[...], yd)):
        t = jnp.dot(part.astype(BF16), w_ref[n * GROUP_W:(n + 1) * GROUP_W, :], preferred_element_type=F32)
        y = t if y is None else y + t
    o_ref[...] = x_ref[...] + _rms(y, pg_ref[...])


def _outproj(ya, yb, yc, od, pd, x, w2, post_g, l, tm):
    t = x.shape[0]
    part = pl.BlockSpec((tm, GROUP_W), lambda i: (i, 0))
    return pl.pallas_call(
        _out_body,
        grid=(t // tm,),
        in_specs=[part, part, part, part, part,
                  pl.BlockSpec((tm, D_MODEL), lambda i: (i, 0)),
                  _layer_spec(w2, l), _layer_spec(post_g, l)],
        out_specs=pl.BlockSpec((tm, D_MODEL), lambda i: (i, 0)),
        out_shape=jax.ShapeDtypeStruct((t, D_MODEL), F32),
        compiler_params=_params("parallel"),
        name="outproj",
    )(ya, yb, yc, od, pd, x, w2, post_g)


def _rot_cols(w):
    half = D_ROPE // 2
    return jnp.concatenate([-w[..., half:], w[..., :half]], axis=-1)


def _prep_w_in(w_in):
    d = w_in.shape[0]
    z = lambda n: jnp.zeros((d, D_MODEL, n), w_in.dtype)
    c = lambda a, n: w_in[:, :, a:a + n]
    kr = c(3204, D_ROPE)
    cols = [c(0, 768),
            c(768, 1024),
            c(1792, 768), c(2564, 256),
            c(3236, 256), c(2820, 256), c(3076, 128),
            c(2560, 4), z(60), kr, z(32),
            z(64), _rot_cols(kr), z(32)]
    return jnp.concatenate(cols, axis=-1).astype(BF16)


def _prep_q(qb_w):
    d = qb_w.shape[0]
    z = lambda n: jnp.zeros((d, D_Q_RANK, n), qb_w.dtype)
    plain, rot = [], []
    for h in range(D_HEADS):
        o = h * (D_NOPE + D_ROPE)
        nope, rope = qb_w[:, :, o:o + D_NOPE], qb_w[:, :, o + D_NOPE:o + D_NOPE + D_ROPE]
        plain += [nope, rope, z(HEAD_W - D_NOPE - D_ROPE)]
        rot += [z(D_NOPE), _rot_cols(rope), z(HEAD_W - D_NOPE - D_ROPE)]
    return jnp.concatenate(plain, axis=-1).astype(BF16), jnp.concatenate(rot, axis=-1).astype(BF16)


def _prep_kv(kvb_w):
    d = kvb_w.shape[0]
    z = lambda n: jnp.zeros((d, D_KV_RANK, n), kvb_w.dtype)
    ks, vs = [], []
    for h in range(D_HEADS):
        o = h * (D_NOPE + D_VDIM)
        ks += [kvb_w[:, :, o:o + D_NOPE], z(HEAD_W - D_NOPE)]
        vs += [kvb_w[:, :, o + D_NOPE:o + D_NOPE + D_VDIM], z(HEAD_W - D_VDIM)]
    return jnp.concatenate(ks, axis=-1).astype(BF16), jnp.concatenate(vs, axis=-1).astype(BF16)


def _row(p, width=None):
    if width is not None and width > p.shape[-1]:
        p = jnp.pad(p, ((0, 0), (0, width - p.shape[-1])))
    return p[:, None, :]


def _pick_tile(n, want):
    t = min(want, n)
    while n % t:
        t //= 2
    return t


@jax.jit
def _forward(x, positions, pre_norm_g, post_norm_g, w_in, w_out, a_dw_w, a_dw_b, a_ln_g, a_ln_b,
             a_pw_w, a_pw_b, b_lb_logits, b_norm_g, c_conv_w, c_conv_b, c_dt_bias, c_a_log, c_d,
             c_norm_g, d_qa_g, d_qb_w, d_kva_g, d_kvb_w):
    batch, seq, _ = x.shape
    depth = w_in.shape[0]
    t = batch * seq
    tm = _pick_tile(seq, 512)
    tb = _pick_tile(seq, 256)

    w1 = _prep_w_in(w_in)
    w2 = w_out.astype(BF16)
    wq, wqr = _prep_q(d_qb_w)
    wk, wv = _prep_kv(d_kvb_w)
    dww = jnp.pad(a_dw_w, ((0, 0), (0, 32 - A_KERNEL), (0, 0)))
    pww = a_pw_w.astype(BF16)
    dsk = _row(jnp.repeat(c_d, C_HEADDIM, axis=-1))
    inv_freq = ROPE_THETA ** (-jnp.arange(0, D_ROPE, 2, dtype=F32) / D_ROPE)
    fr = jnp.concatenate([jnp.zeros((D_NOPE,), F32), inv_freq, inv_freq,
                          jnp.zeros((HEAD_W - D_NOPE - D_ROPE,), F32)])[None, :]
    pos = positions.astype(F32).reshape(t, 1)

    xf = x.reshape(t, D_MODEL)
    for l in range(depth):
        pa, pb, pc, pd, pm = _inproj(xf, _row(pre_norm_g), w1, l, tm)
        ya = _branch_a(pa, dww, _row(a_dw_b), _row(a_ln_g), _row(a_ln_b), pww, _row(a_pw_b), l, seq, tm)
        yb = _branch_b(pb, b_lb_logits, _row(b_norm_g), l, batch, seq, tb)
        yc = _branch_c(pc, pm, c_conv_w, _row(c_conv_b), _row(c_dt_bias, LANES), _row(c_a_log, LANES),
                       dsk, _row(c_norm_g), l, batch, seq, tm)
        q, k, v = _d_prep(pd, pm, pos, _row(d_qa_g), wq, wqr, _row(d_kva_g), wk, wv, fr, l, tm)
        od = _attention(q, k, v, batch, seq, tm)
        xf = _outproj(ya, yb, yc, od, pd, xf, w2, _row(post_norm_g), l, tm)
    return xf.reshape(batch, seq, D_MODEL)


def kernel(x, positions, pre_norm_g, post_norm_g, w_in, w_out, a_dw_w, a_dw_b, a_ln_g, a_ln_b, a_pw_w, a_pw_b, b_lb_logits, b_norm_g, c_conv_w, c_conv_b, c_dt_bias, c_a_log, c_d, c_norm_g, d_qa_g, d_qb_w, d_kva_g, d_kvb_w):
    return _forward(x, positions, pre_norm_g, post_norm_g, w_in, w_out, a_dw_w, a_dw_b, a_ln_g, a_ln_b,
                    a_pw_w, a_pw_b, b_lb_logits, b_norm_g, c_conv_w, c_conv_b, c_dt_bias, c_a_log, c_d,
                    c_norm_g, d_qa_g, d_qb_w, d_kva_g, d_kvb_w)
```

```python
import functools
import math

import jax
import jax.numpy as jnp
from jax import lax
from jax.experimental import pallas as pl
from jax.experimental.pallas import tpu as pltpu

F32 = jnp.float32
BF16 = jnp.bfloat16

D_MODEL = 1024
CHUNK = 64
GROUP_W = 256
NORM_EPS = 1e-6
LN_EPS = 1e-5
MASK_VALUE = -1e30
GATE_FLOOR = 1e-20
A_KERNEL = 31
B_HEADS = 4
B_DK = 64
C_HEADS = 4
C_HEADDIM = 64
C_GROUPS = 2
C_STATE = 128
C_CONV = 4
C_XBC = 768
D_HEADS = 4
D_Q_RANK = 256
D_KV_RANK = 128
D_NOPE = 64
D_ROPE = 32
D_VDIM = 64
ROPE_THETA = 10000.0

LANES = 128
SUB = 16
VMEM_LIMIT = 48 * 1024 * 1024

W_PA, W_PB, W_PC, W_PD, W_PM = 768, 1024, 1024, 640, 256
N_PROJ = W_PA + W_PB + W_PC + W_PD + W_PM

NT = (((1,), (1,)), ((), ()))
TN = (((0,), (0,)), ((), ()))


def _params(*sem):
    return pltpu.CompilerParams(dimension_semantics=sem, vmem_limit_bytes=VMEM_LIMIT)


def _layer_spec(arr, l):
    shape = arr.shape[1:]
    nd = len(shape)
    return pl.BlockSpec((None,) + shape, lambda *_: (l,) + (0,) * nd)


def _split3(x):
    h1 = x.astype(BF16)
    r1 = x - h1.astype(F32)
    h2 = r1.astype(BF16)
    h3 = (r1 - h2.astype(F32)).astype(BF16)
    return h1, h2, h3


def _dot_sel(sel, x, dims=None, sel_on_right=False):
    if dims is None:
        dims = (((1,), (0,)), ((), ()))
    out = None
    for piece in _split3(x):
        a, b = (piece, sel) if sel_on_right else (sel, piece)
        t = lax.dot_general(a, b, dims, preferred_element_type=F32)
        out = t if out is None else out + t
    return out


def _silu(x):
    return x * jax.nn.sigmoid(x)


def _rms(x, g):
    ms = jnp.mean(x * x, axis=-1, keepdims=True)
    return x * lax.rsqrt(ms + NORM_EPS) * g


def _inproj_body(x_ref, g_ref, w_ref, *outs):
    h = _rms(x_ref[...], g_ref[...]).astype(BF16)
    off = 0
    for o in outs:
        n = o.shape[-1]
        o[...] = jnp.dot(h, w_ref[:, off:off + n], preferred_element_type=F32).astype(o.dtype)
        off += n


def _inproj(x, pre_g, w1, l, tm):
    t = x.shape[0]
    widths = (W_PA, W_PB, W_PC, W_PD, W_PM)
    return pl.pallas_call(
        _inproj_body,
        grid=(t // tm,),
        in_specs=[pl.BlockSpec((tm, D_MODEL), lambda i: (i, 0)),
                  _layer_spec(pre_g, l), _layer_spec(w1, l)],
        out_specs=[pl.BlockSpec((tm, n), lambda i: (i, 0)) for n in widths],
        out_shape=[jax.ShapeDtypeStruct((t, n), F32) for n in widths],
        compiler_params=_params("parallel"),
        name="inproj",
    )(x, pre_g, w1)


A_HALO = 32


def _a_body(pa_ref, ph_ref, dww_ref, dwb_ref, lng_ref, lnb_ref, pww_ref, pwb_ref, o_ref,
            hscr, cscr, *, ts, nseq):
    first = (pl.program_id(0) % nseq) == 0
    h = pa_ref[:, 0:256] * jax.nn.sigmoid(pa_ref[:, 256:512])
    hh = ph_ref[:, 0:256] * jax.nn.sigmoid(ph_ref[:, 256:512])
    hscr[0:A_HALO, :] = jnp.where(first, 0.0, hh)
    hscr[A_HALO:A_HALO + ts, :] = h
    w = dww_ref[...]
    rows = 64
    base = A_HALO - (A_KERNEL - 1)
    for r in range(ts // rows):
        acc = jnp.broadcast_to(dwb_ref[...], (rows, GROUP_W))
        for j in range(A_KERNEL):
            s0 = r * rows + base + j
            acc = acc + w[j:j + 1, :] * hscr[s0:s0 + rows, :]
        cscr[r * rows:(r + 1) * rows, :] = acc
    c = cscr[...]
    mu = jnp.mean(c, axis=-1, keepdims=True)
    d = c - mu
    var = jnp.mean(d * d, axis=-1, keepdims=True)
    y = d * lax.rsqrt(var + LN_EPS) * lng_ref[...] + lnb_ref[...]
    y = _silu(y).astype(BF16)
    y = jnp.dot(y, pww_ref[...], preferred_element_type=F32) + pwb_ref[...]
    o_ref[...] = y * _silu(pa_ref[:, 512:768])


def _branch_a(pa, dww, dwb, lng, lnb, pww, pwb, l, seq, ts):
    t = pa.shape[0]
    nseq = seq // ts
    hb = ts // A_HALO
    return pl.pallas_call(
        functools.partial(_a_body, ts=ts, nseq=nseq),
        grid=(t // ts,),
        in_specs=[pl.BlockSpec((ts, W_PA), lambda i: (i, 0)),
                  pl.BlockSpec((A_HALO, W_PA), lambda i: (jnp.maximum(i * hb - 1, 0), 0)),
                  _layer_spec(dww, l), _layer_spec(dwb, l), _layer_spec(lng, l),
                  _layer_spec(lnb, l), _layer_spec(pww, l), _layer_spec(pwb, l)],
        out_specs=pl.BlockSpec((ts, GROUP_W), lambda i: (i, 0)),
        out_shape=jax.ShapeDtypeStruct((t, GROUP_W), F32),
        scratch_shapes=[pltpu.VMEM((A_HALO + ts, GROUP_W), F32),
                        pltpu.VMEM((ts, GROUP_W), F32)],
        compiler_params=_params("arbitrary"),
        name="branch_a",
    )(pa, pa, dww, dwb, lng, lnb, pww, pwb)


def _b_body(pb_ref, lbl_ref, ng_ref, o_ref, st_ref, kscr, bscr, oscr, wscr, *, l, tb):
    @pl.when(pl.program_id(1) == 0)
    def _():
        st_ref[...] = jnp.zeros_like(st_ref)

    lg = lbl_ref[...]
    e = jnp.exp(lg - jnp.max(lg, axis=0, keepdims=True))
    p = e / jnp.sum(e, axis=0, keepdims=True)
    lb = jnp.zeros((1, GROUP_W), F32)
    for m in range(1, l + 1):
        lb = lb + p[m:m + 1, :]

    zf = pb_ref[:, 256:512]
    fg = lb + (1.0 - lb) * jax.nn.sigmoid(zf)
    logf = jnp.log(jnp.maximum(fg, GATE_FLOOR))
    kscr[...] = (1.0 - lb) * jax.nn.sigmoid(-zf)
    ri = lax.broadcasted_iota(jnp.int32, (tb, tb), 0)
    ci = lax.broadcasted_iota(jnp.int32, (tb, tb), 1)
    tril = jnp.where((ri // SUB == ci // SUB) & (ci <= ri), 1.0, 0.0).astype(BF16)
    bscr[...] = _dot_sel(tril, logf)

    hr = lax.broadcasted_iota(jnp.int32, (GROUP_W, GROUP_W), 0) // B_DK
    hc = lax.broadcasted_iota(jnp.int32, (GROUP_W, GROUP_W), 1) // B_DK
    same_head = hr == hc
    bd = jnp.where(same_head, 1.0, 0.0).astype(BF16)
    tio = lax.broadcasted_iota(jnp.int32, (SUB, GROUP_W), 0)

    def sub(jj, carry):
        r0 = pl.multiple_of(jj * SUB, SUB)
        qj = pb_ref[pl.ds(r0, SUB), 0:256]
        vj = pb_ref[pl.ds(r0, SUB), 512:768]
        kj = kscr[pl.ds(r0, SUB), :]
        bj = bscr[pl.ds(r0, SUB), :]
        for s in range(SUB):
            dec = jnp.exp(jnp.minimum(bj - bj[s:s + 1, :], 0.0))
            w = jnp.where(tio >= s, qj * kj[s:s + 1, :] * dec, 0.0)
            wscr[s * SUB:(s + 1) * SUB, :] = w.astype(BF16)
        m = jnp.dot(wscr[...], bd, preferred_element_type=F32)
        o = jnp.zeros((SUB, GROUP_W), F32)
        for s in range(SUB):
            o = o + m[s * SUB:(s + 1) * SUB, :] * vj[s:s + 1, :]
        st = st_ref[...]
        qt = (qj * jnp.exp(bj)).astype(BF16)
        o = o + lax.dot_general(qt, st.astype(BF16), NT, preferred_element_type=F32)
        oscr[pl.ds(r0, SUB), :] = o
        blast = bj[SUB - 1:SUB, :]
        kh = (kj * jnp.exp(blast - bj)).astype(BF16)
        u = lax.dot_general(vj.astype(BF16), kh, TN, preferred_element_type=F32)
        st_ref[...] = st * jnp.exp(blast) + jnp.where(same_head, u, 0.0)
        return carry

    lax.fori_loop(0, tb // SUB, sub, 0)

    o = oscr[...]
    ms = _dot_sel(bd, o * o, sel_on_right=True) * (1.0 / B_DK)
    y = o * lax.rsqrt(ms + NORM_EPS) * ng_ref[...]
    o_ref[...] = y * _silu(pb_ref[:, 768:1024])


def _branch_b(pb, lbl, ng, l, batch, seq, tb):
    t = pb.shape[0]
    nseq = seq // tb
    return pl.pallas_call(
        functools.partial(_b_body, l=l, tb=tb),
        grid=(batch, nseq),
        in_specs=[pl.BlockSpec((tb, W_PB), lambda b, j: (b * nseq + j, 0)),
                  pl.BlockSpec(lbl.shape, lambda b, j: (0, 0)),
                  _layer_spec(ng, l)],
        out_specs=pl.BlockSpec((tb, GROUP_W), lambda b, j: (b * nseq + j, 0)),
        out_shape=jax.ShapeDtypeStruct((t, GROUP_W), F32),
        scratch_shapes=[pltpu.VMEM((GROUP_W, GROUP_W), F32),
                        pltpu.VMEM((tb, GROUP_W), F32),
                        pltpu.VMEM((tb, GROUP_W), F32),
                        pltpu.VMEM((tb, GROUP_W), F32),
                        pltpu.VMEM((SUB * SUB, GROUP_W), BF16)],
        compiler_params=_params("arbitrary", "arbitrary"),
        name="branch_b",
    )(pb, lbl, ng)


C_HALO = 8


def _c_body(pc_ref, ph_ref, pm_ref, cw_ref, cb_ref, dtb_ref, alog_ref, dsk_ref, ng_ref, o_ref,
            h_ref, xscr, cscr, ascr, dscr, yscr, *, tc):
    first = pl.program_id(1) == 0

    @pl.when(first)
    def _():
        h_ref[...] = jnp.zeros_like(h_ref)

    xscr[0:C_HALO, :] = jnp.where(first, 0.0, ph_ref[:, 0:C_XBC])
    xscr[C_HALO:C_HALO + tc, :] = pc_ref[:, 0:C_XBC]
    w = cw_ref[...]
    rows = 64
    base = C_HALO - (C_CONV - 1)
    for r in range(tc // rows):
        acc = jnp.broadcast_to(cb_ref[...], (rows, C_XBC))
        for j in range(C_CONV):
            s0 = r * rows + base + j
            acc = acc + w[j:j + 1, :] * xscr[s0:s0 + rows, :]
        cscr[r * rows:(r + 1) * rows, :] = _silu(acc)

    dt = jnp.logaddexp(pm_ref[:, 0:LANES] + dtb_ref[...], 0.0)
    dscr[...] = dt
    a = dt * (-jnp.exp(alog_ref[...]))
    ri = lax.broadcasted_iota(jnp.int32, (tc, tc), 0)
    ci = lax.broadcasted_iota(jnp.int32, (tc, tc), 1)
    tril = jnp.where((ri // CHUNK == ci // CHUNK) & (ci <= ri), 1.0, 0.0).astype(BF16)
    ascr[...] = _dot_sel(tril, a)

    sel = jnp.where(lax.broadcasted_iota(jnp.int32, (8, LANES), 0)
                    == lax.broadcasted_iota(jnp.int32, (8, LANES), 1), 1.0, 0.0).astype(BF16)
    li = lax.broadcasted_iota(jnp.int32, (CHUNK, CHUNK), 0)
    si = lax.broadcasted_iota(jnp.int32, (CHUNK, CHUNK), 1)
    causal = li >= si
    dsk = dsk_ref[...]
    e = C_HEADS // C_GROUPS

    def chunk(c, carry):
        r0 = pl.multiple_of(c * CHUNK, CHUNK)
        acs = ascr[pl.ds(r0, CHUNK), :]
        dtc = dscr[pl.ds(r0, CHUNK), :]
        arow = _dot_sel(sel, acs, NT)
        xs = cscr[pl.ds(r0, CHUNK), 0:GROUP_W]
        for g in range(C_GROUPS):
            bmg = cscr[pl.ds(r0, CHUNK), GROUP_W + g * C_STATE:GROUP_W + (g + 1) * C_STATE].astype(BF16)
            c0 = GROUP_W + C_GROUPS * C_STATE + g * C_STATE
            cmg = cscr[pl.ds(r0, CHUNK), c0:c0 + C_STATE].astype(BF16)
            cb = lax.dot_general(cmg, bmg, NT, preferred_element_type=F32)
            for ee in range(e):
                h = g * e + ee
                col = acs[:, h:h + 1]
                row = arow[h:h + 1, :]
                lm = jnp.where(causal, jnp.exp(jnp.minimum(col - row, 0.0)), 0.0)
                xsh = xs[:, h * C_HEADDIM:(h + 1) * C_HEADDIM]
                xdt = xsh * dtc[:, h:h + 1]
                ydiag = jnp.dot((cb * lm).astype(BF16), xdt.astype(BF16), preferred_element_type=F32)
                hprev = h_ref[h * C_HEADDIM:(h + 1) * C_HEADDIM, :]
                yoff = lax.dot_general(cmg, hprev.astype(BF16), NT, preferred_element_type=F32)
                yoff = yoff * jnp.exp(col)
                alast = row[:, CHUNK - 1:CHUNK]
                wx = (xdt * jnp.exp(alast - col)).astype(BF16)
                stt = lax.dot_general(wx, bmg, TN, preferred_element_type=F32)
                h_ref[h * C_HEADDIM:(h + 1) * C_HEADDIM, :] = jnp.exp(alast) * hprev + stt
                yscr[pl.ds(r0, CHUNK), h * C_HEADDIM:(h + 1) * C_HEADDIM] = (
                    ydiag + yoff + xsh * dsk[:, h * C_HEADDIM:(h + 1) * C_HEADDIM])
        return carry

    lax.fori_loop(0, tc // CHUNK, chunk, 0)

    yz = yscr[...] * _silu(pc_ref[:, C_XBC:C_XBC + GROUP_W])
    gw = e * C_HEADDIM
    for g in range(C_GROUPS):
        o_ref[:, g * gw:(g + 1) * gw] = _rms(yz[:, g * gw:(g + 1) * gw], ng_ref[:, g * gw:(g + 1) * gw])


def _branch_c(pc, pm, cw, cb, dtb, alog, dsk, ng, l, batch, seq, tc):
    t = pc.shape[0]
    nseq = seq // tc
    hb = tc // C_HALO
    return pl.pallas_call(
        functools.partial(_c_body, tc=tc),
        grid=(batch, nseq),
        in_specs=[pl.BlockSpec((tc, W_PC), lambda b, j: (b * nseq + j, 0)),
                  pl.BlockSpec((C_HALO, W_PC), lambda b, j: (jnp.maximum((b * nseq + j) * hb - 1, 0), 0)),
                  pl.BlockSpec((tc, W_PM), lambda b, j: (b * nseq + j, 0)),
                  _layer_spec(cw, l), _layer_spec(cb, l), _layer_spec(dtb, l),
                  _layer_spec(alog, l), _layer_spec(dsk, l), _layer_spec(ng, l)],
        out_specs=pl.BlockSpec((tc, GROUP_W), lambda b, j: (b * nseq + j, 0)),
        out_shape=jax.ShapeDtypeStruct((t, GROUP_W), F32),
        scratch_shapes=[pltpu.VMEM((C_HEADS * C_HEADDIM, C_STATE), F32),
                        pltpu.VMEM((C_HALO + tc, C_XBC), F32),
                        pltpu.VMEM((tc, C_XBC), F32),
                        pltpu.VMEM((tc, LANES), F32),
                        pltpu.VMEM((tc, LANES), F32),
                        pltpu.VMEM((tc, GROUP_W), F32)],
        compiler_params=_params("arbitrary", "arbitrary"),
        name="branch_c",
    )(pc, pc, pm, cw, cb, dtb, alog, dsk, ng)


HEAD_W = 128
V_ROWS = 80
QK_SCALE = (D_NOPE + D_ROPE) ** -0.5
LOG2E = 1.4426950408889634


def _d_prep_body(pd_ref, pm_ref, posc_ref, posr_ref, qag_ref, wqt_ref, wqrt_ref, kvag_ref, wk_ref, wvt_ref,
                 fr_ref, frc_ref, qt_ref, k_ref, vt_ref):
    sc = QK_SCALE * LOG2E
    hq = _rms(pd_ref[:, 256:512], qag_ref[...]).astype(BF16)
    q0 = lax.dot_general(wqt_ref[...], hq, NT, preferred_element_type=F32)
    q1 = lax.dot_general(wqrt_ref[...], hq, NT, preferred_element_type=F32)
    angt = frc_ref[...] * posr_ref[...]
    cst = jnp.cos(angt)
    snt = jnp.sin(angt)
    for h in range(D_HEADS):
        r0 = h * HEAD_W
        qt_ref[h, 0:D_NOPE, :] = (q0[r0:r0 + D_NOPE] * sc).astype(BF16)
        rope = q0[r0 + D_NOPE:r0 + D_NOPE + D_ROPE] * cst + q1[h * D_ROPE:(h + 1) * D_ROPE] * snt
        qt_ref[h, D_NOPE:D_NOPE + D_ROPE, :] = (rope * sc).astype(BF16)
        qt_ref[h, D_NOPE + D_ROPE:HEAD_W, :] = jnp.zeros((HEAD_W - D_NOPE - D_ROPE, q0.shape[1]), BF16)

    hkv = _rms(pd_ref[:, 512:640], kvag_ref[...]).astype(BF16)
    k0 = jnp.dot(hkv, wk_ref[...], preferred_element_type=F32)
    ang = posc_ref[...] * fr_ref[...]
    cs = jnp.cos(ang)
    sn = jnp.sin(ang)
    lane = lax.broadcasted_iota(jnp.int32, ang.shape, 1)
    rope_lane = (lane >= D_NOPE) & (lane < D_NOPE + D_ROPE)
    kr = jnp.where(rope_lane, pm_ref[:, 0:LANES] * cs, 0.0) + pm_ref[:, LANES:2 * LANES] * sn
    k_ref[...] = (k0 + jnp.concatenate([kr] * D_HEADS, axis=1)).astype(BF16)

    vt = lax.dot_general(wvt_ref[...], hkv, NT, preferred_element_type=F32)
    pad = V_ROWS - D_VDIM
    ones_row = jnp.where(lax.broadcasted_iota(jnp.int32, (pad, vt.shape[1]), 0) == 0, 1.0, 0.0).astype(BF16)
    for h in range(D_HEADS):
        vt_ref[h, 0:D_VDIM, :] = vt[h * D_VDIM:(h + 1) * D_VDIM].astype(BF16)
        vt_ref[h, D_VDIM:V_ROWS, :] = ones_row


def _d_prep(pd, pm, posc, posr, qag, wqt, wqrt, kvag, wk, wvt, fr, frc, l, batch, seq, tm):
    t = pd.shape[0]
    nseq = seq // tm
    return pl.pallas_call(
        _d_prep_body,
        grid=(batch, nseq),
        in_specs=[pl.BlockSpec((tm, W_PD), lambda b, i: (b * nseq + i, 0)),
                  pl.BlockSpec((tm, W_PM), lambda b, i: (b * nseq + i, 0)),
                  pl.BlockSpec((tm, 1), lambda b, i: (b * nseq + i, 0)),
                  pl.BlockSpec((1, tm), lambda b, i: (0, b * nseq + i)),
                  _layer_spec(qag, l), _layer_spec(wqt, l), _layer_spec(wqrt, l),
                  _layer_spec(kvag, l), _layer_spec(wk, l), _layer_spec(wvt, l),
                  pl.BlockSpec(fr.shape, lambda b, i: (0, 0)),
                  pl.BlockSpec(frc.shape, lambda b, i: (0, 0))],
        out_specs=[pl.BlockSpec((None, D_HEADS, HEAD_W, tm), lambda b, i: (b, 0, 0, i)),
                   pl.BlockSpec((tm, D_HEADS * HEAD_W), lambda b, i: (b * nseq + i, 0)),
                   pl.BlockSpec((None, D_HEADS, None, V_ROWS, tm), lambda b, i: (b, 0, i, 0, 0))],
        out_shape=[jax.ShapeDtypeStruct((batch, D_HEADS, HEAD_W, seq), BF16),
                   jax.ShapeDtypeStruct((t, D_HEADS * HEAD_W), BF16),
                   jax.ShapeDtypeStruct((batch, D_HEADS, nseq, V_ROWS, tm), BF16)],
        compiler_params=_params("parallel", "parallel"),
        name="mla_prep",
    )(pd, pm, posc, posr, qag, wqt, wqrt, kvag, wk, wvt, fr, frc)


ATT_HEADS_PER_STEP = 2


def _attn_body(qt_ref, k_ref, vt_ref, o_ref, acc, mrow, sa, sb, *, tq):
    i = pl.program_id(2)
    key_chunk = lax.broadcasted_iota(jnp.int32, (tq, tq), 0) // CHUNK
    qry_chunk = lax.broadcasted_iota(jnp.int32, (tq, tq), 1) // CHUNK
    visible = key_chunk <= qry_chunk
    outs = []
    for hh in range(ATT_HEADS_PER_STEP):
        lanes = slice(hh * HEAD_W, (hh + 1) * HEAD_W)
        qt = qt_ref[hh]

        def scores(j):
            k0 = pl.multiple_of(j * tq, tq)
            return jnp.dot(k_ref[pl.ds(k0, tq), lanes], qt, preferred_element_type=F32)

        def fold(st, j):
            m_old = mrow[...]
            m_new = jnp.maximum(m_old, jnp.max(st, axis=0, keepdims=True))
            pt = jnp.exp2(st - m_new).astype(BF16)
            acc[...] = jnp.exp2(m_old - m_new) * acc[...] + jnp.dot(vt_ref[hh, j], pt, preferred_element_type=F32)
            mrow[...] = m_new

        st = jnp.where(visible, scores(i), MASK_VALUE)
        sa[...] = scores(0)
        m0 = jnp.max(st, axis=0, keepdims=True)
        mrow[...] = m0
        acc[...] = jnp.dot(vt_ref[hh, i], jnp.exp2(st - m0).astype(BF16), preferred_element_type=F32)

        pairs = lax.shift_right_logical(jnp.maximum(i - 1, 0), 1)

        def pipelined(p, carry):
            j = 2 * p
            sb[...] = scores(j + 1)
            fold(sa[...], j)
            sa[...] = scores(j + 2)
            fold(sb[...], j + 1)
            return carry

        lax.fori_loop(0, pairs, pipelined, 0)
        j0 = 2 * pairs
        left = i - j0

        @pl.when(left == 2)
        def _():
            sb[...] = scores(j0 + 1)
            fold(sa[...], j0)
            fold(sb[...], j0 + 1)

        @pl.when(left == 1)
        def _():
            fold(sa[...], j0)

        a = acc[...]
        outs.append(a[0:D_VDIM, :] / a[D_VDIM:D_VDIM + 1, :])
    o_ref[...] = jnp.concatenate(outs, axis=0).T


def _attention(qt, k, vt, batch, seq, tq):
    t = k.shape[0]
    nq = seq // tq
    g = ATT_HEADS_PER_STEP
    return pl.pallas_call(
        functools.partial(_attn_body, tq=tq),
        grid=(batch, D_HEADS // g, nq),
        in_specs=[pl.BlockSpec((None, g, HEAD_W, tq), lambda b, h, i: (b, h, 0, i)),
                  pl.BlockSpec((seq, g * HEAD_W), lambda b, h, i: (b, h)),
                  pl.BlockSpec((None, g, nq, V_ROWS, tq), lambda b, h, i: (b, h, 0, 0, 0))],
        out_specs=pl.BlockSpec((tq, g * D_VDIM), lambda b, h, i: (b * nq + i, h)),
        out_shape=jax.ShapeDtypeStruct((t, D_HEADS * D_VDIM), F32),
        scratch_shapes=[pltpu.VMEM((V_ROWS, tq), F32), pltpu.VMEM((1, tq), F32),
                        pltpu.VMEM((tq, tq), F32), pltpu.VMEM((tq, tq), F32)],
        compiler_params=_params("arbitrary", "arbitrary", "arbitrary"),
        name="mla_attention",
    )(qt, k, vt)


def _out_body(ya_ref, yb_ref, yc_ref, od_ref, gd_ref, x_ref, w_ref, pg_ref, o_ref):
    yd = od_ref[...] * _silu(gd_ref[...])
    y = None
    for n, part in enumerate((ya_ref[...], yb_ref[...], yc_ref[...], yd)):
        t = jnp.dot(part.astype(BF16), w_ref[n * GROUP_W:(n + 1) * GROUP_W, :], preferred_element_type=F32)
        y = t if y is None else y + t
    o_ref[...] = x_ref[...] + _rms(y, pg_ref[...])


def _outproj(ya, yb, yc, od, pd, x, w2, post_g, l, tm):
    t = x.shape[0]
    part = pl.BlockSpec((tm, GROUP_W), lambda i: (i, 0))
    return pl.pallas_call(
        _out_body,
        grid=(t // tm,),
        in_specs=[part, part, part, part, part,
                  pl.BlockSpec((tm, D_MODEL), lambda i: (i, 0)),
                  _layer_spec(w2, l), _layer_spec(post_g, l)],
        out_specs=pl.BlockSpec((tm, D_MODEL), lambda i: (i, 0)),
        out_shape=jax.ShapeDtypeStruct((t, D_MODEL), F32),
        compiler_params=_params("parallel"),
        name="outproj",
    )(ya, yb, yc, od, pd, x, w2, post_g)


def _rot_cols(w):
    half = D_ROPE // 2
    return jnp.concatenate([-w[..., half:], w[..., :half]], axis=-1)


def _prep_w_in(w_in):
    d = w_in.shape[0]
    z = lambda n: jnp.zeros((d, D_MODEL, n), w_in.dtype)
    c = lambda a, n: w_in[:, :, a:a + n]
    kr = c(3204, D_ROPE)
    cols = [c(0, 768),
            c(768, 1024),
            c(1792, 768), c(2564, 256),
            c(3236, 256), c(2820, 256), c(3076, 128),
            c(2560, 4), z(60), kr, z(32),
            z(64), _rot_cols(kr), z(32)]
    return jnp.concatenate(cols, axis=-1).astype(BF16)


def _prep_q(qb_w):
    d = qb_w.shape[0]
    z = lambda n: jnp.zeros((d, D_Q_RANK, n), qb_w.dtype)
    plain, rot = [], []
    for h in range(D_HEADS):
        o = h * (D_NOPE + D_ROPE)
        nope, rope = qb_w[:, :, o:o + D_NOPE], qb_w[:, :, o + D_NOPE:o + D_NOPE + D_ROPE]
        plain += [nope, rope, z(HEAD_W - D_NOPE - D_ROPE)]
        rot += [_rot_cols(rope)]
    t = lambda cols: jnp.swapaxes(jnp.concatenate(cols, axis=-1), 1, 2).astype(BF16)
    return t(plain), t(rot)


def _prep_kv(kvb_w):
    d = kvb_w.shape[0]
    z = lambda n: jnp.zeros((d, D_KV_RANK, n), kvb_w.dtype)
    ks, vs = [], []
    for h in range(D_HEADS):
        o = h * (D_NOPE + D_VDIM)
        ks += [kvb_w[:, :, o:o + D_NOPE], z(HEAD_W - D_NOPE)]
        vs += [kvb_w[:, :, o + D_NOPE:o + D_NOPE + D_VDIM]]
    return (jnp.concatenate(ks, axis=-1).astype(BF16),
            jnp.swapaxes(jnp.concatenate(vs, axis=-1), 1, 2).astype(BF16))


def _row(p, width=None):
    if width is not None and width > p.shape[-1]:
        p = jnp.pad(p, ((0, 0), (0, width - p.shape[-1])))
    return p[:, None, :]


def _pick_tile(n, want):
    t = min(want, n)
    while n % t:
        t //= 2
    return t


@jax.jit
def _forward(x, positions, pre_norm_g, post_norm_g, w_in, w_out, a_dw_w, a_dw_b, a_ln_g, a_ln_b,
             a_pw_w, a_pw_b, b_lb_logits, b_norm_g, c_conv_w, c_conv_b, c_dt_bias, c_a_log, c_d,
             c_norm_g, d_qa_g, d_qb_w, d_kva_g, d_kvb_w):
    batch, seq, _ = x.shape
    depth = w_in.shape[0]
    t = batch * seq
    tm = _pick_tile(seq, 512)
    tb = _pick_tile(seq, 256)

    w1 = _prep_w_in(w_in)
    w2 = w_out.astype(BF16)
    wqt, wqrt = _prep_q(d_qb_w)
    wk, wvt = _prep_kv(d_kvb_w)
    dww = jnp.pad(a_dw_w, ((0, 0), (0, 32 - A_KERNEL), (0, 0)))
    pww = a_pw_w.astype(BF16)
    dsk = _row(jnp.repeat(c_d, C_HEADDIM, axis=-1))
    inv_freq = ROPE_THETA ** (-jnp.arange(0, D_ROPE, 2, dtype=F32) / D_ROPE)
    fr = jnp.concatenate([jnp.zeros((D_NOPE,), F32), inv_freq, inv_freq,
                          jnp.zeros((HEAD_W - D_NOPE - D_ROPE,), F32)])[None, :]
    frc = jnp.concatenate([inv_freq, inv_freq])[:, None]
    posc = positions.astype(F32).reshape(t, 1)
    posr = positions.astype(F32).reshape(1, t)

    xf = x.reshape(t, D_MODEL)
    for l in range(depth):
        pa, pb, pc, pd, pm = _inproj(xf, _row(pre_norm_g), w1, l, tm)
        ya = _branch_a(pa, dww, _row(a_dw_b), _row(a_ln_g), _row(a_ln_b), pww, _row(a_pw_b), l, seq, tm)
        yb = _branch_b(pb, b_lb_logits, _row(b_norm_g), l, batch, seq, tb)
        yc = _branch_c(pc, pm, c_conv_w, _row(c_conv_b), _row(c_dt_bias, LANES), _row(c_a_log, LANES),
                       dsk, _row(c_norm_g), l, batch, seq, tm)
        qt, k, vt = _d_prep(pd, pm, posc, posr, _row(d_qa_g), wqt, wqrt, _row(d_kva_g), wk, wvt, fr, frc,
                            l, batch, seq, tm)
        od = _attention(qt, k, vt, batch, seq, tm)
        xf = _outproj(ya, yb, yc, od, pd, xf, w2, _row(post_norm_g), l, tm)
    return xf.reshape(batch, seq, D_MODEL)


def kernel(x, positions, pre_norm_g, post_norm_g, w_in, w_out, a_dw_w, a_dw_b, a_ln_g, a_ln_b, a_pw_w, a_pw_b, b_lb_logits, b_norm_g, c_conv_w, c_conv_b, c_dt_bias, c_a_log, c_d, c_norm_g, d_qa_g, d_qb_w, d_kva_g, d_kvb_w):
    return _forward(x, positions, pre_norm_g, post_norm_g, w_in, w_out, a_dw_w, a_dw_b, a_ln_g, a_ln_b,
                    a_pw_w, a_pw_b, b_lb_logits, b_norm_g, c_conv_w, c_conv_b, c_dt_bias, c_a_log, c_d,
                    c_norm_g, d_qa_g, d_qb_w, d_kva_g, d_kvb_w)
```

```python
import functools
import math

import jax
import jax.numpy as jnp
from jax import lax
from jax.experimental import pallas as pl
from jax.experimental.pallas import tpu as pltpu

F32 = jnp.float32
BF16 = jnp.bfloat16

D_MODEL = 1024
CHUNK = 64
GROUP_W = 256
NORM_EPS = 1e-6
LN_EPS = 1e-5
MASK_VALUE = -1e30
GATE_FLOOR = 1e-20
A_KERNEL = 31
B_HEADS = 4
B_DK = 64
C_HEADS = 4
C_HEADDIM = 64
C_GROUPS = 2
C_STATE = 128
C_CONV = 4
C_XBC = 768
D_HEADS = 4
D_Q_RANK = 256
D_KV_RANK = 128
D_NOPE = 64
D_ROPE = 32
D_VDIM = 64
ROPE_THETA = 10000.0

LANES = 128
SUB = 16
VMEM_LIMIT = 48 * 1024 * 1024

W_PA, W_PB, W_PC, W_PD, W_PM = 768, 1024, 1024, 640, 256
N_PROJ = W_PA + W_PB + W_PC + W_PD + W_PM

NT = (((1,), (1,)), ((), ()))
TN = (((0,), (0,)), ((), ()))


def _params(*sem):
    return pltpu.CompilerParams(dimension_semantics=sem, vmem_limit_bytes=VMEM_LIMIT)


def _layer_spec(arr, l):
    shape = arr.shape[1:]
    nd = len(shape)
    return pl.BlockSpec((None,) + shape, lambda *_: (l,) + (0,) * nd)


def _split3(x):
    h1 = x.astype(BF16)
    r1 = x - h1.astype(F32)
    h2 = r1.astype(BF16)
    h3 = (r1 - h2.astype(F32)).astype(BF16)
    return h1, h2, h3


def _dot_sel(sel, x, dims=None, sel_on_right=False):
    if dims is None:
        dims = (((1,), (0,)), ((), ()))
    out = None
    for piece in _split3(x):
        a, b = (piece, sel) if sel_on_right else (sel, piece)
        t = lax.dot_general(a, b, dims, preferred_element_type=F32)
        out = t if out is None else out + t
    return out


def _silu(x):
    return x * jax.nn.sigmoid(x)


def _rms(x, g):
    ms = jnp.mean(x * x, axis=-1, keepdims=True)
    return x * lax.rsqrt(ms + NORM_EPS) * g


def _inproj_body(x_ref, g_ref, w_ref, *outs):
    h = _rms(x_ref[...], g_ref[...]).astype(BF16)
    off = 0
    for o in outs:
        n = o.shape[-1]
        o[...] = jnp.dot(h, w_ref[:, off:off + n], preferred_element_type=F32).astype(o.dtype)
        off += n


def _inproj(x, pre_g, w1, l, tm):
    t = x.shape[0]
    widths = (W_PA, W_PB, W_PC, W_PD, W_PM)
    return pl.pallas_call(
        _inproj_body,
        grid=(t // tm,),
        in_specs=[pl.BlockSpec((tm, D_MODEL), lambda i: (i, 0)),
                  _layer_spec(pre_g, l), _layer_spec(w1, l)],
        out_specs=[pl.BlockSpec((tm, n), lambda i: (i, 0)) for n in widths],
        out_shape=[jax.ShapeDtypeStruct((t, n), F32) for n in widths],
        compiler_params=_params("parallel"),
        name="inproj",
    )(x, pre_g, w1)


A_HALO = 32


def _a_body(pa_ref, ph_ref, dww_ref, dwb_ref, lng_ref, lnb_ref, pww_ref, pwb_ref, o_ref,
            hscr, cscr, *, ts, nseq):
    first = (pl.program_id(0) % nseq) == 0
    h = pa_ref[:, 0:256] * jax.nn.sigmoid(pa_ref[:, 256:512])
    hh = ph_ref[:, 0:256] * jax.nn.sigmoid(ph_ref[:, 256:512])
    hscr[0:A_HALO, :] = jnp.where(first, 0.0, hh)
    hscr[A_HALO:A_HALO + ts, :] = h
    w = dww_ref[...]
    rows = 64
    base = A_HALO - (A_KERNEL - 1)
    for r in range(ts // rows):
        acc = jnp.broadcast_to(dwb_ref[...], (rows, GROUP_W))
        for j in range(A_KERNEL):
            s0 = r * rows + base + j
            acc = acc + w[j:j + 1, :] * hscr[s0:s0 + rows, :]
        cscr[r * rows:(r + 1) * rows, :] = acc
    c = cscr[...]
    mu = jnp.mean(c, axis=-1, keepdims=True)
    d = c - mu
    var = jnp.mean(d * d, axis=-1, keepdims=True)
    y = d * lax.rsqrt(var + LN_EPS) * lng_ref[...] + lnb_ref[...]
    y = _silu(y).astype(BF16)
    y = jnp.dot(y, pww_ref[...], preferred_element_type=F32) + pwb_ref[...]
    o_ref[...] = y * _silu(pa_ref[:, 512:768])


def _branch_a(pa, dww, dwb, lng, lnb, pww, pwb, l, seq, ts):
    t = pa.shape[0]
    nseq = seq // ts
    hb = ts // A_HALO
    return pl.pallas_call(
        functools.partial(_a_body, ts=ts, nseq=nseq),
        grid=(t // ts,),
        in_specs=[pl.BlockSpec((ts, W_PA), lambda i: (i, 0)),
                  pl.BlockSpec((A_HALO, W_PA), lambda i: (jnp.maximum(i * hb - 1, 0), 0)),
                  _layer_spec(dww, l), _layer_spec(dwb, l), _layer_spec(lng, l),
                  _layer_spec(lnb, l), _layer_spec(pww, l), _layer_spec(pwb, l)],
        out_specs=pl.BlockSpec((ts, GROUP_W), lambda i: (i, 0)),
        out_shape=jax.ShapeDtypeStruct((t, GROUP_W), F32),
        scratch_shapes=[pltpu.VMEM((A_HALO + ts, GROUP_W), F32),
                        pltpu.VMEM((ts, GROUP_W), F32)],
        compiler_params=_params("arbitrary"),
        name="branch_a",
    )(pa, pa, dww, dwb, lng, lnb, pww, pwb)


def _group_tril(n, group):
    r = jnp.arange(n)[:, None]
    c = jnp.arange(n)[None, :]
    return ((r // group == c // group) & (c <= r)).astype(BF16)


def _b_body(pb_ref, lbl_ref, ng_ref, tril_ref, o_ref, st_ref, kscr, bscr, oscr, wscr, *, l, tb):
    @pl.when(pl.program_id(1) == 0)
    def _():
        st_ref[...] = jnp.zeros_like(st_ref)

    lg = lbl_ref[...]
    e = jnp.exp(lg - jnp.max(lg, axis=0, keepdims=True))
    p = e / jnp.sum(e, axis=0, keepdims=True)
    lb = jnp.zeros((1, GROUP_W), F32)
    for m in range(1, l + 1):
        lb = lb + p[m:m + 1, :]

    zf = pb_ref[:, 256:512]
    fg = lb + (1.0 - lb) * jax.nn.sigmoid(zf)
    logf = jnp.log(jnp.maximum(fg, GATE_FLOOR))
    kscr[...] = (1.0 - lb) * jax.nn.sigmoid(-zf)
    bscr[...] = _dot_sel(tril_ref[...], logf) * LOG2E

    hr = lax.broadcasted_iota(jnp.int32, (GROUP_W, GROUP_W), 0) // B_DK
    hc = lax.broadcasted_iota(jnp.int32, (GROUP_W, GROUP_W), 1) // B_DK
    same_head = hr == hc
    bd = jnp.where(same_head, 1.0, 0.0).astype(BF16)
    half = SUB // 2
    tio = lax.broadcasted_iota(jnp.int32, (SUB, GROUP_W), 0)
    tio_hi = lax.broadcasted_iota(jnp.int32, (half, GROUP_W), 0) + half
    zero_lo = jnp.zeros((half, GROUP_W), F32)

    def sub(jj, carry):
        r0 = pl.multiple_of(jj * SUB, SUB)
        qj = pb_ref[pl.ds(r0, SUB), 0:256]
        vj = pb_ref[pl.ds(r0, SUB), 512:768]
        kj = kscr[pl.ds(r0, SUB), :]
        bj = bscr[pl.ds(r0, SUB), :]
        for s in range(SUB):
            if s < half:
                w = jnp.where(tio >= s, qj * kj[s:s + 1, :] * jnp.exp2(bj - bj[s:s + 1, :]), 0.0)
            else:
                w = jnp.where(tio_hi >= s, qj[half:] * kj[s:s + 1, :] * jnp.exp2(bj[half:] - bj[s:s + 1, :]), 0.0)
                w = jnp.concatenate([zero_lo, w], axis=0)
            wscr[s * SUB:(s + 1) * SUB, :] = w.astype(BF16)
        m = jnp.dot(wscr[...], bd, preferred_element_type=F32)
        o_lo = jnp.zeros((half, GROUP_W), F32)
        o_hi = jnp.zeros((half, GROUP_W), F32)
        for s in range(SUB):
            vrow = vj[s:s + 1, :]
            if s < half:
                o_lo = o_lo + m[s * SUB:s * SUB + half, :] * vrow
            o_hi = o_hi + m[s * SUB + half:(s + 1) * SUB, :] * vrow
        o = jnp.concatenate([o_lo, o_hi], axis=0)
        st = st_ref[...]
        qt = (qj * jnp.exp2(bj)).astype(BF16)
        o = o + lax.dot_general(qt, st.astype(BF16), NT, preferred_element_type=F32)
        oscr[pl.ds(r0, SUB), :] = o
        blast = bj[SUB - 1:SUB, :]
        kh = (kj * jnp.exp2(blast - bj)).astype(BF16)
        u = lax.dot_general(vj.astype(BF16), kh, TN, preferred_element_type=F32)
        st_ref[...] = st * jnp.exp2(blast) + jnp.where(same_head, u, 0.0)
        return carry

    lax.fori_loop(0, tb // SUB, sub, 0, unroll=2)

    o = oscr[...]
    ms = _dot_sel(bd, o * o, sel_on_right=True) * (1.0 / B_DK)
    y = o * lax.rsqrt(ms + NORM_EPS) * ng_ref[...]
    o_ref[...] = y * _silu(pb_ref[:, 768:1024])


def _branch_b(pb, lbl, ng, l, batch, seq, tb):
    t = pb.shape[0]
    nseq = seq // tb
    return pl.pallas_call(
        functools.partial(_b_body, l=l, tb=tb),
        grid=(batch, nseq),
        in_specs=[pl.BlockSpec((tb, W_PB), lambda b, j: (b * nseq + j, 0)),
                  pl.BlockSpec(lbl.shape, lambda b, j: (0, 0)),
                  _layer_spec(ng, l),
                  pl.BlockSpec((tb, tb), lambda b, j: (0, 0))],
        out_specs=pl.BlockSpec((tb, GROUP_W), lambda b, j: (b * nseq + j, 0)),
        out_shape=jax.ShapeDtypeStruct((t, GROUP_W), F32),
        scratch_shapes=[pltpu.VMEM((GROUP_W, GROUP_W), F32),
                        pltpu.VMEM((tb, GROUP_W), F32),
                        pltpu.VMEM((tb, GROUP_W), F32),
                        pltpu.VMEM((tb, GROUP_W), F32),
                        pltpu.VMEM((SUB * SUB, GROUP_W), BF16)],
        compiler_params=_params("arbitrary", "arbitrary"),
        name="branch_b",
    )(pb, lbl, ng, _group_tril(tb, SUB))


C_HALO = 8


def _c_body(pc_ref, ph_ref, pm_ref, cw_ref, cb_ref, dtb_ref, alog_ref, dsk_ref, ng_ref, tril_ref, o_ref,
            h_ref, xscr, cscr, ascr, dscr, yscr, *, tc):
    first = pl.program_id(1) == 0

    @pl.when(first)
    def _():
        h_ref[...] = jnp.zeros_like(h_ref)

    xscr[0:C_HALO, :] = jnp.where(first, 0.0, ph_ref[:, 0:C_XBC])
    xscr[C_HALO:C_HALO + tc, :] = pc_ref[:, 0:C_XBC]
    w = cw_ref[...]
    rows = 64
    base = C_HALO - (C_CONV - 1)
    for r in range(tc // rows):
        acc = jnp.broadcast_to(cb_ref[...], (rows, C_XBC))
        for j in range(C_CONV):
            s0 = r * rows + base + j
            acc = acc + w[j:j + 1, :] * xscr[s0:s0 + rows, :]
        cscr[r * rows:(r + 1) * rows, :] = _silu(acc)

    dt = jnp.logaddexp(pm_ref[:, 0:LANES] + dtb_ref[...], 0.0)
    dscr[...] = dt
    a = dt * (-jnp.exp(alog_ref[...]))
    ascr[...] = _dot_sel(tril_ref[...], a)

    sel = jnp.where(lax.broadcasted_iota(jnp.int32, (8, LANES), 0)
                    == lax.broadcasted_iota(jnp.int32, (8, LANES), 1), 1.0, 0.0).astype(BF16)
    li = lax.broadcasted_iota(jnp.int32, (CHUNK, CHUNK), 0)
    si = lax.broadcasted_iota(jnp.int32, (CHUNK, CHUNK), 1)
    causal = li >= si
    dsk = dsk_ref[...]
    e = C_HEADS // C_GROUPS

    def chunk(c, carry):
        r0 = pl.multiple_of(c * CHUNK, CHUNK)
        acs = ascr[pl.ds(r0, CHUNK), :]
        dtc = dscr[pl.ds(r0, CHUNK), :]
        arow = _dot_sel(sel, acs, NT)
        xs = cscr[pl.ds(r0, CHUNK), 0:GROUP_W]
        for g in range(C_GROUPS):
            bmg = cscr[pl.ds(r0, CHUNK), GROUP_W + g * C_STATE:GROUP_W + (g + 1) * C_STATE].astype(BF16)
            c0 = GROUP_W + C_GROUPS * C_STATE + g * C_STATE
            cmg = cscr[pl.ds(r0, CHUNK), c0:c0 + C_STATE].astype(BF16)
            cb = lax.dot_general(cmg, bmg, NT, preferred_element_type=F32)
            for ee in range(e):
                h = g * e + ee
                col = acs[:, h:h + 1]
                row = arow[h:h + 1, :]
                lm = jnp.where(causal, jnp.exp(jnp.minimum(col - row, 0.0)), 0.0)
                xsh = xs[:, h * C_HEADDIM:(h + 1) * C_HEADDIM]
                xdt = xsh * dtc[:, h:h + 1]
                ydiag = jnp.dot((cb * lm).astype(BF16), xdt.astype(BF16), preferred_element_type=F32)
                hprev = h_ref[h * C_HEADDIM:(h + 1) * C_HEADDIM, :]
                yoff = lax.dot_general(cmg, hprev.astype(BF16), NT, preferred_element_type=F32)
                yoff = yoff * jnp.exp(col)
                alast = row[:, CHUNK - 1:CHUNK]
                wx = (xdt * jnp.exp(alast - col)).astype(BF16)
                stt = lax.dot_general(wx, bmg, TN, preferred_element_type=F32)
                h_ref[h * C_HEADDIM:(h + 1) * C_HEADDIM, :] = jnp.exp(alast) * hprev + stt
                yscr[pl.ds(r0, CHUNK), h * C_HEADDIM:(h + 1) * C_HEADDIM] = (
                    ydiag + yoff + xsh * dsk[:, h * C_HEADDIM:(h + 1) * C_HEADDIM])
        return carry

    lax.fori_loop(0, tc // CHUNK, chunk, 0, unroll=2)

    yz = yscr[...] * _silu(pc_ref[:, C_XBC:C_XBC + GROUP_W])
    gw = e * C_HEADDIM
    for g in range(C_GROUPS):
        o_ref[:, g * gw:(g + 1) * gw] = _rms(yz[:, g * gw:(g + 1) * gw], ng_ref[:, g * gw:(g + 1) * gw])


def _branch_c(pc, pm, cw, cb, dtb, alog, dsk, ng, l, batch, seq, tc):
    t = pc.shape[0]
    nseq = seq // tc
    hb = tc // C_HALO
    return pl.pallas_call(
        functools.partial(_c_body, tc=tc),
        grid=(batch, nseq),
        in_specs=[pl.BlockSpec((tc, W_PC), lambda b, j: (b * nseq + j, 0)),
                  pl.BlockSpec((C_HALO, W_PC), lambda b, j: (jnp.maximum((b * nseq + j) * hb - 1, 0), 0)),
                  pl.BlockSpec((tc, W_PM), lambda b, j: (b * nseq + j, 0)),
                  _layer_spec(cw, l), _layer_spec(cb, l), _layer_spec(dtb, l),
                  _layer_spec(alog, l), _layer_spec(dsk, l), _layer_spec(ng, l),
                  pl.BlockSpec((tc, tc), lambda b, j: (0, 0))],
        out_specs=pl.BlockSpec((tc, GROUP_W), lambda b, j: (b * nseq + j, 0)),
        out_shape=jax.ShapeDtypeStruct((t, GROUP_W), F32),
        scratch_shapes=[pltpu.VMEM((C_HEADS * C_HEADDIM, C_STATE), F32),
                        pltpu.VMEM((C_HALO + tc, C_XBC), F32),
                        pltpu.VMEM((tc, C_XBC), F32),
                        pltpu.VMEM((tc, LANES), F32),
                        pltpu.VMEM((tc, LANES), F32),
                        pltpu.VMEM((tc, GROUP_W), F32)],
        compiler_params=_params("arbitrary", "arbitrary"),
        name="branch_c",
    )(pc, pc, pm, cw, cb, dtb, alog, dsk, ng, _group_tril(tc, CHUNK))


HEAD_W = 128
V_ROWS = 80
QK_SCALE = (D_NOPE + D_ROPE) ** -0.5
LOG2E = 1.4426950408889634


def _d_prep_body(pd_ref, pm_ref, posc_ref, posr_ref, qag_ref, wqt_ref, wqrt_ref, kvag_ref, wk_ref, wvt_ref,
                 fr_ref, frc_ref, qt_ref, k_ref, vt_ref):
    sc = QK_SCALE * LOG2E
    hq = _rms(pd_ref[:, 256:512], qag_ref[...]).astype(BF16)
    q0 = lax.dot_general(wqt_ref[...], hq, NT, preferred_element_type=F32)
    q1 = lax.dot_general(wqrt_ref[...], hq, NT, preferred_element_type=F32)
    angt = frc_ref[...] * posr_ref[...]
    cst = jnp.cos(angt)
    snt = jnp.sin(angt)
    for h in range(D_HEADS):
        r0 = h * HEAD_W
        qt_ref[h, 0:D_NOPE, :] = (q0[r0:r0 + D_NOPE] * sc).astype(BF16)
        rope = q0[r0 + D_NOPE:r0 + D_NOPE + D_ROPE] * cst + q1[h * D_ROPE:(h + 1) * D_ROPE] * snt
        qt_ref[h, D_NOPE:D_NOPE + D_ROPE, :] = (rope * sc).astype(BF16)
        qt_ref[h, D_NOPE + D_ROPE:HEAD_W, :] = jnp.zeros((HEAD_W - D_NOPE - D_ROPE, q0.shape[1]), BF16)

    hkv = _rms(pd_ref[:, 512:640], kvag_ref[...]).astype(BF16)
    k0 = jnp.dot(hkv, wk_ref[...], preferred_element_type=F32)
    ang = posc_ref[...] * fr_ref[...]
    cs = jnp.cos(ang)
    sn = jnp.sin(ang)
    lane = lax.broadcasted_iota(jnp.int32, ang.shape, 1)
    rope_lane = (lane >= D_NOPE) & (lane < D_NOPE + D_ROPE)
    kr = jnp.where(rope_lane, pm_ref[:, 0:LANES] * cs, 0.0) + pm_ref[:, LANES:2 * LANES] * sn
    k_ref[...] = (k0 + jnp.concatenate([kr] * D_HEADS, axis=1)).astype(BF16)

    vt = lax.dot_general(wvt_ref[...], hkv, NT, preferred_element_type=F32)
    pad = V_ROWS - D_VDIM
    ones_row = jnp.where(lax.broadcasted_iota(jnp.int32, (pad, vt.shape[1]), 0) == 0, 1.0, 0.0).astype(BF16)
    for h in range(D_HEADS):
        vt_ref[h, 0:D_VDIM, :] = vt[h * D_VDIM:(h + 1) * D_VDIM].astype(BF16)
        vt_ref[h, D_VDIM:V_ROWS, :] = ones_row


def _d_prep(pd, pm, posc, posr, qag, wqt, wqrt, kvag, wk, wvt, fr, frc, l, batch, seq, tm):
    t = pd.shape[0]
    nseq = seq // tm
    return pl.pallas_call(
        _d_prep_body,
        grid=(batch, nseq),
        in_specs=[pl.BlockSpec((tm, W_PD), lambda b, i: (b * nseq + i, 0)),
                  pl.BlockSpec((tm, W_PM), lambda b, i: (b * nseq + i, 0)),
                  pl.BlockSpec((tm, 1), lambda b, i: (b * nseq + i, 0)),
                  pl.BlockSpec((1, tm), lambda b, i: (0, b * nseq + i)),
                  _layer_spec(qag, l), _layer_spec(wqt, l), _layer_spec(wqrt, l),
                  _layer_spec(kvag, l), _layer_spec(wk, l), _layer_spec(wvt, l),
                  pl.BlockSpec(fr.shape, lambda b, i: (0, 0)),
                  pl.BlockSpec(frc.shape, lambda b, i: (0, 0))],
        out_specs=[pl.BlockSpec((None, D_HEADS, HEAD_W, tm), lambda b, i: (b, 0, 0, i)),
                   pl.BlockSpec((tm, D_HEADS * HEAD_W), lambda b, i: (b * nseq + i, 0)),
                   pl.BlockSpec((None, D_HEADS, None, V_ROWS, tm), lambda b, i: (b, 0, i, 0, 0))],
        out_shape=[jax.ShapeDtypeStruct((batch, D_HEADS, HEAD_W, seq), BF16),
                   jax.ShapeDtypeStruct((t, D_HEADS * HEAD_W), BF16),
                   jax.ShapeDtypeStruct((batch, D_HEADS, nseq, V_ROWS, tm), BF16)],
        compiler_params=_params("parallel", "parallel"),
        name="mla_prep",
    )(pd, pm, posc, posr, qag, wqt, wqrt, kvag, wk, wvt, fr, frc)


ATT_HEADS_PER_STEP = 2


def _attn_body(qt_ref, k_ref, vt_ref, o_ref, acc, mrow, sa, sb, *, tq):
    i = pl.program_id(2)
    key_chunk = lax.broadcasted_iota(jnp.int32, (tq, tq), 0) // CHUNK
    qry_chunk = lax.broadcasted_iota(jnp.int32, (tq, tq), 1) // CHUNK
    visible = key_chunk <= qry_chunk
    outs = []
    for hh in range(ATT_HEADS_PER_STEP):
        lanes = slice(hh * HEAD_W, (hh + 1) * HEAD_W)
        qt = qt_ref[hh]

        def scores(kt):
            k0 = pl.multiple_of(kt * tq, tq)
            return jnp.dot(k_ref[pl.ds(k0, tq), lanes], qt, preferred_element_type=F32)

        def fold(st, j):
            m_old = mrow[...]
            m_new = jnp.maximum(m_old, jnp.max(st, axis=0, keepdims=True))
            pt = jnp.exp2(st - m_new).astype(BF16)
            acc[...] = jnp.exp2(m_old - m_new) * acc[...] + jnp.dot(vt_ref[hh, j], pt, preferred_element_type=F32)
            mrow[...] = m_new

        st = jnp.where(visible, scores(i), MASK_VALUE)
        sa[...] = scores(0)
        m0 = jnp.max(st, axis=0, keepdims=True)
        mrow[...] = m0
        acc[...] = jnp.dot(vt_ref[hh, i], jnp.exp2(st - m0).astype(BF16), preferred_element_type=F32)

        def two_tiles(j):
            sb[...] = scores(j + 1)
            fold(sa[...], j)
            sa[...] = scores(j + 2)
            fold(sb[...], j + 1)

        def quad_trip(p, carry):
            two_tiles(4 * p)
            two_tiles(4 * p + 2)
            return carry

        quads = lax.shift_right_logical(jnp.maximum(i - 1, 0), 2)
        lax.fori_loop(0, quads, quad_trip, 0)
        j1 = 4 * quads

        def pair_trip(p, carry):
            two_tiles(j1 + 2 * p)
            return carry

        pairs = lax.shift_right_logical(jnp.maximum(i - 1 - j1, 0), 1)
        lax.fori_loop(0, pairs, pair_trip, 0)
        j0 = j1 + 2 * pairs
        left = i - j0

        @pl.when(left == 2)
        def _():
            sb[...] = scores(j0 + 1)
            fold(sa[...], j0)
            fold(sb[...], j0 + 1)

        @pl.when(left == 1)
        def _():
            fold(sa[...], j0)

        a = acc[...]
        outs.append(a[0:D_VDIM, :] / a[D_VDIM:D_VDIM + 1, :])
    o_ref[...] = jnp.concatenate(outs, axis=0).T


def _attention(qt, k, vt, batch, seq, tq):
    t = k.shape[0]
    nq = seq // tq
    g = ATT_HEADS_PER_STEP
    return pl.pallas_call(
        functools.partial(_attn_body, tq=tq),
        grid=(batch, D_HEADS // g, nq),
        in_specs=[pl.BlockSpec((None, g, HEAD_W, tq), lambda b, h, i: (b, h, 0, i)),
                  pl.BlockSpec((seq, g * HEAD_W), lambda b, h, i: (b, h)),
                  pl.BlockSpec((None, g, nq, V_ROWS, tq), lambda b, h, i: (b, h, 0, 0, 0))],
        out_specs=pl.BlockSpec((tq, g * D_VDIM), lambda b, h, i: (b * nq + i, h)),
        out_shape=jax.ShapeDtypeStruct((t, D_HEADS * D_VDIM), F32),
        scratch_shapes=[pltpu.VMEM((V_ROWS, tq), F32), pltpu.VMEM((1, tq), F32),
                        pltpu.VMEM((tq, tq), F32), pltpu.VMEM((tq, tq), F32)],
        compiler_params=_params("arbitrary", "arbitrary", "arbitrary"),
        name="mla_attention",
    )(qt, k, vt)


def _out_body(ya_ref, yb_ref, yc_ref, od_ref, gd_ref, x_ref, w_ref, pg_ref, o_ref):
    yd = od_ref[...] * _silu(gd_ref[...])
    y = None
    for n, part in enumerate((ya_ref[...], yb_ref[...], yc_ref[...], yd)):
        t = jnp.dot(part.astype(BF16), w_ref[n * GROUP_W:(n + 1) * GROUP_W, :], preferred_element_type=F32)
        y = t if y is None else y + t
    o_ref[...] = x_ref[...] + _rms(y, pg_ref[...])


def _outproj(ya, yb, yc, od, pd, x, w2, post_g, l, tm):
    t = x.shape[0]
    part = pl.BlockSpec((tm, GROUP_W), lambda i: (i, 0))
    return pl.pallas_call(
        _out_body,
        grid=(t // tm,),
        in_specs=[part, part, part, part, part,
                  pl.BlockSpec((tm, D_MODEL), lambda i: (i, 0)),
                  _layer_spec(w2, l), _layer_spec(post_g, l)],
        out_specs=pl.BlockSpec((tm, D_MODEL), lambda i: (i, 0)),
        out_shape=jax.ShapeDtypeStruct((t, D_MODEL), F32),
        compiler_params=_params("parallel"),
        name="outproj",
    )(ya, yb, yc, od, pd, x, w2, post_g)


def _rot_cols(w):
    half = D_ROPE // 2
    return jnp.concatenate([-w[..., half:], w[..., :half]], axis=-1)


def _prep_w_in(w_in):
    d = w_in.shape[0]
    z = lambda n: jnp.zeros((d, D_MODEL, n), w_in.dtype)
    c = lambda a, n: w_in[:, :, a:a + n]
    kr = c(3204, D_ROPE)
    cols = [c(0, 768),
            c(768, 1024),
            c(1792, 768), c(2564, 256),
            c(3236, 256), c(2820, 256), c(3076, 128),
            c(2560, 4), z(60), kr, z(32),
            z(64), _rot_cols(kr), z(32)]
    return jnp.concatenate(cols, axis=-1).astype(BF16)


def _prep_q(qb_w):
    d = qb_w.shape[0]
    z = lambda n: jnp.zeros((d, D_Q_RANK, n), qb_w.dtype)
    plain, rot = [], []
    for h in range(D_HEADS):
        o = h * (D_NOPE + D_ROPE)
        nope, rope = qb_w[:, :, o:o + D_NOPE], qb_w[:, :, o + D_NOPE:o + D_NOPE + D_ROPE]
        plain += [nope, rope, z(HEAD_W - D_NOPE - D_ROPE)]
        rot += [_rot_cols(rope)]
    t = lambda cols: jnp.swapaxes(jnp.concatenate(cols, axis=-1), 1, 2).astype(BF16)
    return t(plain), t(rot)


def _prep_kv(kvb_w):
    d = kvb_w.shape[0]
    z = lambda n: jnp.zeros((d, D_KV_RANK, n), kvb_w.dtype)
    ks, vs = [], []
    for h in range(D_HEADS):
        o = h * (D_NOPE + D_VDIM)
        ks += [kvb_w[:, :, o:o + D_NOPE], z(HEAD_W - D_NOPE)]
        vs += [kvb_w[:, :, o + D_NOPE:o + D_NOPE + D_VDIM]]
    return (jnp.concatenate(ks, axis=-1).astype(BF16),
            jnp.swapaxes(jnp.concatenate(vs, axis=-1), 1, 2).astype(BF16))


def _row(p, width=None):
    if width is not None and width > p.shape[-1]:
        p = jnp.pad(p, ((0, 0), (0, width - p.shape[-1])))
    return p[:, None, :]


def _pick_tile(n, want):
    t = min(want, n)
    while n % t:
        t //= 2
    return t


@jax.jit
def _forward(x, positions, pre_norm_g, post_norm_g, w_in, w_out, a_dw_w, a_dw_b, a_ln_g, a_ln_b,
             a_pw_w, a_pw_b, b_lb_logits, b_norm_g, c_conv_w, c_conv_b, c_dt_bias, c_a_log, c_d,
             c_norm_g, d_qa_g, d_qb_w, d_kva_g, d_kvb_w):
    batch, seq, _ = x.shape
    depth = w_in.shape[0]
    t = batch * seq
    tm = _pick_tile(seq, 512)
    tb = _pick_tile(seq, 256)

    w1 = _prep_w_in(w_in)
    w2 = w_out.astype(BF16)
    wqt, wqrt = _prep_q(d_qb_w)
    wk, wvt = _prep_kv(d_kvb_w)
    dww = jnp.pad(a_dw_w, ((0, 0), (0, 32 - A_KERNEL), (0, 0)))
    pww = a_pw_w.astype(BF16)
    dsk = _row(jnp.repeat(c_d, C_HEADDIM, axis=-1))
    inv_freq = ROPE_THETA ** (-jnp.arange(0, D_ROPE, 2, dtype=F32) / D_ROPE)
    fr = jnp.concatenate([jnp.zeros((D_NOPE,), F32), inv_freq, inv_freq,
                          jnp.zeros((HEAD_W - D_NOPE - D_ROPE,), F32)])[None, :]
    frc = jnp.concatenate([inv_freq, inv_freq])[:, None]
    posc = positions.astype(F32).reshape(t, 1)
    posr = positions.astype(F32).reshape(1, t)

    xf = x.reshape(t, D_MODEL)
    for l in range(depth):
        pa, pb, pc, pd, pm = _inproj(xf, _row(pre_norm_g), w1, l, tm)
        ya = _branch_a(pa, dww, _row(a_dw_b), _row(a_ln_g), _row(a_ln_b), pww, _row(a_pw_b), l, seq, tm)
        yb = _branch_b(pb, b_lb_logits, _row(b_norm_g), l, batch, seq, tb)
        yc = _branch_c(pc, pm, c_conv_w, _row(c_conv_b), _row(c_dt_bias, LANES), _row(c_a_log, LANES),
                       dsk, _row(c_norm_g), l, batch, seq, tm)
        qt, k, vt = _d_prep(pd, pm, posc, posr, _row(d_qa_g), wqt, wqrt, _row(d_kva_g), wk, wvt, fr, frc,
                            l, batch, seq, tm)
        od = _attention(qt, k, vt, batch, seq, tm)
        xf = _outproj(ya, yb, yc, od, pd, xf, w2, _row(post_norm_g), l, tm)
    return xf.reshape(batch, seq, D_MODEL)


def kernel(x, positions, pre_norm_g, post_norm_g, w_in, w_out, a_dw_w, a_dw_b, a_ln_g, a_ln_b, a_pw_w, a_pw_b, b_lb_logits, b_norm_g, c_conv_w, c_conv_b, c_dt_bias, c_a_log, c_d, c_norm_g, d_qa_g, d_qb_w, d_kva_g, d_kvb_w):
    return _forward(x, positions, pre_norm_g, post_norm_g, w_in, w_out, a_dw_w, a_dw_b, a_ln_g, a_ln_b,
                    a_pw_w, a_pw_b, b_lb_logits, b_norm_g, c_conv_w, c_conv_b, c_dt_bias, c_a_log, c_d,
                    c_norm_g, d_qa_g, d_qb_w, d_kva_g, d_kvb_w)
```

```python
import functools
import math

import jax
import jax.numpy as jnp
from jax import lax
from jax.experimental import pallas as pl
from jax.experimental.pallas import tpu as pltpu

F32 = jnp.float32
BF16 = jnp.bfloat16

D_MODEL = 1024
CHUNK = 64
GROUP_W = 256
NORM_EPS = 1e-6
LN_EPS = 1e-5
MASK_VALUE = -1e30
GATE_FLOOR = 1e-20
A_KERNEL = 31
B_HEADS = 4
B_DK = 64
C_HEADS = 4
C_HEADDIM = 64
C_GROUPS = 2
C_STATE = 128
C_CONV = 4
C_XBC = 768
D_HEADS = 4
D_Q_RANK = 256
D_KV_RANK = 128
D_NOPE = 64
D_ROPE = 32
D_VDIM = 64
ROPE_THETA = 10000.0

LANES = 128
SUB = 16
VMEM_LIMIT = 48 * 1024 * 1024

W_PA, W_PB, W_PC, W_PD, W_PM = 768, 1024, 1024, 640, 256
N_PROJ = W_PA + W_PB + W_PC + W_PD + W_PM

NT = (((1,), (1,)), ((), ()))
TN = (((0,), (0,)), ((), ()))


def _params(*sem):
    return pltpu.CompilerParams(dimension_semantics=sem, vmem_limit_bytes=VMEM_LIMIT)


def _layer_spec(arr, l):
    shape = arr.shape[1:]
    nd = len(shape)
    return pl.BlockSpec((None,) + shape, lambda *_: (l,) + (0,) * nd)


def _split3(x):
    h1 = x.astype(BF16)
    r1 = x - h1.astype(F32)
    h2 = r1.astype(BF16)
    h3 = (r1 - h2.astype(F32)).astype(BF16)
    return h1, h2, h3


def _dot_sel(sel, x, dims=None, sel_on_right=False):
    if dims is None:
        dims = (((1,), (0,)), ((), ()))
    out = None
    for piece in _split3(x):
        a, b = (piece, sel) if sel_on_right else (sel, piece)
        t = lax.dot_general(a, b, dims, preferred_element_type=F32)
        out = t if out is None else out + t
    return out


def _silu(x):
    return x * jax.nn.sigmoid(x)


def _rms(x, g):
    ms = jnp.mean(x * x, axis=-1, keepdims=True)
    return x * lax.rsqrt(ms + NORM_EPS) * g


def _inproj_body(x_ref, g_ref, w_ref, *outs):
    h = _rms(x_ref[...], g_ref[...]).astype(BF16)
    off = 0
    for o in outs:
        n = o.shape[-1]
        o[...] = jnp.dot(h, w_ref[:, off:off + n], preferred_element_type=F32).astype(o.dtype)
        off += n


def _inproj(x, pre_g, w1, l, tm):
    t = x.shape[0]
    widths = (W_PA, W_PB, W_PC, W_PD, W_PM)
    return pl.pallas_call(
        _inproj_body,
        grid=(t // tm,),
        in_specs=[pl.BlockSpec((tm, D_MODEL), lambda i: (i, 0)),
                  _layer_spec(pre_g, l), _layer_spec(w1, l)],
        out_specs=[pl.BlockSpec((tm, n), lambda i: (i, 0)) for n in widths],
        out_shape=[jax.ShapeDtypeStruct((t, n), F32) for n in widths],
        compiler_params=_params("parallel"),
        name="inproj",
    )(x, pre_g, w1)


A_HALO = 32


def _a_body(pa_ref, ph_ref, dww_ref, dwb_ref, lng_ref, lnb_ref, pww_ref, pwb_ref, o_ref,
            hscr, cscr, *, ts, nseq):
    first = (pl.program_id(0) % nseq) == 0
    h = pa_ref[:, 0:256] * jax.nn.sigmoid(pa_ref[:, 256:512])
    hh = ph_ref[:, 0:256] * jax.nn.sigmoid(ph_ref[:, 256:512])
    hscr[0:A_HALO, :] = jnp.where(first, 0.0, hh)
    hscr[A_HALO:A_HALO + ts, :] = h
    w = dww_ref[...]
    rows = 64
    base = A_HALO - (A_KERNEL - 1)
    for r in range(ts // rows):
        acc = jnp.broadcast_to(dwb_ref[...], (rows, GROUP_W))
        for j in range(A_KERNEL):
            s0 = r * rows + base + j
            acc = acc + w[j:j + 1, :] * hscr[s0:s0 + rows, :]
        cscr[r * rows:(r + 1) * rows, :] = acc
    c = cscr[...]
    mu = jnp.mean(c, axis=-1, keepdims=True)
    d = c - mu
    var = jnp.mean(d * d, axis=-1, keepdims=True)
    y = d * lax.rsqrt(var + LN_EPS) * lng_ref[...] + lnb_ref[...]
    y = _silu(y).astype(BF16)
    y = jnp.dot(y, pww_ref[...], preferred_element_type=F32) + pwb_ref[...]
    o_ref[...] = y * _silu(pa_ref[:, 512:768])


def _branch_a(pa, dww, dwb, lng, lnb, pww, pwb, l, seq, ts):
    t = pa.shape[0]
    nseq = seq // ts
    hb = ts // A_HALO
    return pl.pallas_call(
        functools.partial(_a_body, ts=ts, nseq=nseq),
        grid=(t // ts,),
        in_specs=[pl.BlockSpec((ts, W_PA), lambda i: (i, 0)),
                  pl.BlockSpec((A_HALO, W_PA), lambda i: (jnp.maximum(i * hb - 1, 0), 0)),
                  _layer_spec(dww, l), _layer_spec(dwb, l), _layer_spec(lng, l),
                  _layer_spec(lnb, l), _layer_spec(pww, l), _layer_spec(pwb, l)],
        out_specs=pl.BlockSpec((ts, GROUP_W), lambda i: (i, 0)),
        out_shape=jax.ShapeDtypeStruct((t, GROUP_W), F32),
        scratch_shapes=[pltpu.VMEM((A_HALO + ts, GROUP_W), F32),
                        pltpu.VMEM((ts, GROUP_W), F32)],
        compiler_params=_params("arbitrary"),
        name="branch_a",
    )(pa, pa, dww, dwb, lng, lnb, pww, pwb)


def _group_tril(n, group):
    r = jnp.arange(n)[:, None]
    c = jnp.arange(n)[None, :]
    return ((r // group == c // group) & (c <= r)).astype(BF16)


def _b_body(pb_ref, lbl_ref, ng_ref, tril_ref, o_ref, st_ref, kscr, bscr, oscr, *, l, tb):
    @pl.when(pl.program_id(1) == 0)
    def _():
        st_ref[...] = jnp.zeros_like(st_ref)

    lg = lbl_ref[...]
    e = jnp.exp(lg - jnp.max(lg, axis=0, keepdims=True))
    p = e / jnp.sum(e, axis=0, keepdims=True)
    lb = jnp.zeros((1, GROUP_W), F32)
    for m in range(1, l + 1):
        lb = lb + p[m:m + 1, :]

    zf = pb_ref[:, 256:512]
    fg = lb + (1.0 - lb) * jax.nn.sigmoid(zf)
    logf = jnp.log(jnp.maximum(fg, GATE_FLOOR))
    kscr[...] = (1.0 - lb) * jax.nn.sigmoid(-zf)
    bscr[...] = _dot_sel(tril_ref[...], logf) * LOG2E

    hr = lax.broadcasted_iota(jnp.int32, (GROUP_W, GROUP_W), 0) // B_DK
    hc = lax.broadcasted_iota(jnp.int32, (GROUP_W, GROUP_W), 1) // B_DK
    same_head = hr == hc
    bd = jnp.where(same_head, 1.0, 0.0).astype(BF16)
    half = SUB // 2
    tio = lax.broadcasted_iota(jnp.int32, (SUB, GROUP_W), 0)
    tio_hi = lax.broadcasted_iota(jnp.int32, (half, GROUP_W), 0) + half
    zero_lo = jnp.zeros((half, GROUP_W), F32)
    head_rows = (lax.broadcasted_iota(jnp.int32, (B_HEADS * SUB, GROUP_W), 0) // SUB
                 == lax.broadcasted_iota(jnp.int32, (B_HEADS * SUB, GROUP_W), 1) // B_DK)

    def intra(qj, kj, vj, bj):
        ws = []
        for s in range(SUB):
            if s < half:
                w = jnp.where(tio >= s, qj * kj[s:s + 1, :] * jnp.exp2(bj - bj[s:s + 1, :]), 0.0)
            else:
                w = jnp.where(tio_hi >= s, qj[half:] * kj[s:s + 1, :] * jnp.exp2(bj[half:] - bj[s:s + 1, :]), 0.0)
                w = jnp.concatenate([zero_lo, w], axis=0)
            ws.append(w.astype(BF16))
        m = jnp.dot(jnp.concatenate(ws, axis=0), bd, preferred_element_type=F32)
        o_lo = jnp.zeros((half, GROUP_W), F32)
        o_hi = jnp.zeros((half, GROUP_W), F32)
        for s in range(SUB):
            vrow = vj[s:s + 1, :]
            if s < half:
                o_lo = o_lo + m[s * SUB:s * SUB + half, :] * vrow
            o_hi = o_hi + m[s * SUB + half:(s + 1) * SUB, :] * vrow
        return jnp.concatenate([o_lo, o_hi], axis=0)

    def state_step(st, qj, kj, vj, bj):
        qt = (qj * jnp.exp2(bj)).astype(BF16)
        o = lax.dot_general(qt, st.astype(BF16), NT, preferred_element_type=F32)
        blast = bj[SUB - 1:SUB, :]
        kh = (kj * jnp.exp2(blast - bj)).astype(BF16)
        u = lax.dot_general(vj.astype(BF16), kh, TN, preferred_element_type=F32)
        return o, st * jnp.exp2(blast) + jnp.where(same_head, u, 0.0)

    group = 8

    def trip(jj, carry):
        r0 = pl.multiple_of(jj * (group * SUB), group * SUB)
        rows = pl.ds(r0, group * SUB)
        qg, vg = pb_ref[rows, 0:256], pb_ref[rows, 512:768]
        kg, bg = kscr[rows, :], bscr[rows, :]
        st = st_ref[...]
        parts = [slice(n * SUB, (n + 1) * SUB) for n in range(group)]
        outs = [intra(qg[p], kg[p], vg[p], bg[p]) for p in parts]
        for n, p in enumerate(parts):
            o_inter, st = state_step(st, qg[p], kg[p], vg[p], bg[p])
            outs[n] = outs[n] + o_inter
        oscr[rows, :] = jnp.concatenate(outs, axis=0)
        st_ref[...] = st
        return carry

    lax.fori_loop(0, tb // (group * SUB), trip, 0)

    o = oscr[...]
    ms = _dot_sel(bd, o * o, sel_on_right=True) * (1.0 / B_DK)
    y = o * lax.rsqrt(ms + NORM_EPS) * ng_ref[...]
    o_ref[...] = y * _silu(pb_ref[:, 768:1024])


def _branch_b(pb, lbl, ng, l, batch, seq, tb):
    t = pb.shape[0]
    nseq = seq // tb
    return pl.pallas_call(
        functools.partial(_b_body, l=l, tb=tb),
        grid=(batch, nseq),
        in_specs=[pl.BlockSpec((tb, W_PB), lambda b, j: (b * nseq + j, 0)),
                  pl.BlockSpec(lbl.shape, lambda b, j: (0, 0)),
                  _layer_spec(ng, l),
                  pl.BlockSpec((tb, tb), lambda b, j: (0, 0))],
        out_specs=pl.BlockSpec((tb, GROUP_W), lambda b, j: (b * nseq + j, 0)),
        out_shape=jax.ShapeDtypeStruct((t, GROUP_W), F32),
        scratch_shapes=[pltpu.VMEM((GROUP_W, GROUP_W), F32),
                        pltpu.VMEM((tb, GROUP_W), F32),
                        pltpu.VMEM((tb, GROUP_W), F32),
                        pltpu.VMEM((tb, GROUP_W), F32)],
        compiler_params=_params("arbitrary", "arbitrary"),
        name="branch_b",
    )(pb, lbl, ng, _group_tril(tb, SUB))


C_HALO = 8


def _c_body(pc_ref, ph_ref, pm_ref, cw_ref, cb_ref, dtb_ref, alog_ref, dsk_ref, ng_ref, tril_ref, exp_ref,
            o_ref, h_ref, xscr, cscr, aescr, xdscr, yscr, *, tc):
    first = pl.program_id(1) == 0

    @pl.when(first)
    def _():
        h_ref[...] = jnp.zeros_like(h_ref)

    xscr[0:C_HALO, :] = jnp.where(first, 0.0, ph_ref[:, 0:C_XBC])
    xscr[C_HALO:C_HALO + tc, :] = pc_ref[:, 0:C_XBC]
    rows = 64
    base = C_HALO - (C_CONV - 1)
    for lb in range(C_XBC // GROUP_W):
        ls = slice(lb * GROUP_W, (lb + 1) * GROUP_W)
        w = cw_ref[:, ls]
        bias = jnp.broadcast_to(cb_ref[:, ls], (rows, GROUP_W))
        for r in range(tc // rows):
            acc = bias
            for j in range(C_CONV):
                s0 = r * rows + base + j
                acc = acc + w[j:j + 1, :] * xscr[s0:s0 + rows, ls]
            cscr[r * rows:(r + 1) * rows, ls] = _silu(acc)

    dt = jnp.logaddexp(pm_ref[:, 0:LANES] + dtb_ref[...], 0.0)
    dte = _dot_sel(exp_ref[...], dt, sel_on_right=True)
    xdscr[...] = cscr[:, 0:GROUP_W] * dte
    a = dte * (-jnp.exp(alog_ref[...]))
    tb = tril_ref.shape[0]
    for r in range(tc // tb):
        aescr[r * tb:(r + 1) * tb, :] = _dot_sel(tril_ref[...], a[r * tb:(r + 1) * tb, :])

    e = C_HEADS // C_GROUPS
    gw = e * C_HEADDIM
    sel = jnp.where(lax.broadcasted_iota(jnp.int32, (8, GROUP_W), 0) * C_HEADDIM
                    == lax.broadcasted_iota(jnp.int32, (8, GROUP_W), 1), 1.0, 0.0).astype(BF16)
    li = lax.broadcasted_iota(jnp.int32, (CHUNK, gw), 0)
    lane = lax.broadcasted_iota(jnp.int32, (CHUNK, gw), 1)
    causal = li >= lane % C_HEADDIM
    first_head = lane < C_HEADDIM
    dsk = dsk_ref[...]

    def chunk(c, carry):
        r0 = pl.multiple_of(c * CHUNK, CHUNK)
        rs = pl.ds(r0, CHUNK)
        ae = aescr[rs, :]
        alast = ae[CHUNK - 1:CHUNK, :]
        arow = _dot_sel(sel, ae, NT)
        xs = cscr[rs, 0:GROUP_W]
        xdt = xdscr[rs, :]
        wx = xdt * jnp.exp(alast - ae)
        for g in range(C_GROUPS):
            ls = slice(g * gw, (g + 1) * gw)
            bmg = cscr[rs, GROUP_W + g * C_STATE:GROUP_W + (g + 1) * C_STATE].astype(BF16)
            c0 = GROUP_W + C_GROUPS * C_STATE + g * C_STATE
            cmg = cscr[rs, c0:c0 + C_STATE].astype(BF16)
            cb2 = lax.dot_general(cmg, jnp.concatenate([bmg, bmg], axis=0), NT, preferred_element_type=F32)
            rowpair = jnp.concatenate([arow[g * e:g * e + 1, :], arow[g * e + 1:g * e + 2, :]], axis=1)
            lm2 = jnp.where(causal, jnp.exp(jnp.minimum(ae[:, ls] - rowpair, 0.0)), 0.0)
            xg = xdt[:, ls]
            x2 = jnp.concatenate([jnp.where(first_head, xg, 0.0), jnp.where(first_head, 0.0, xg)], axis=0)
            ydiag = jnp.dot((cb2 * lm2).astype(BF16), x2.astype(BF16), preferred_element_type=F32)
            ht = h_ref[g * C_STATE:(g + 1) * C_STATE, :]
            yoff = jnp.dot(cmg, ht.astype(BF16), preferred_element_type=F32) * jnp.exp(ae[:, ls])
            stt = lax.dot_general(bmg, wx[:, ls].astype(BF16), TN, preferred_element_type=F32)
            h_ref[g * C_STATE:(g + 1) * C_STATE, :] = ht * jnp.exp(alast[:, ls]) + stt
            yscr[rs, ls] = ydiag + yoff + xs[:, ls] * dsk[:, ls]
        return carry

    lax.fori_loop(0, tc // CHUNK, chunk, 0, unroll=2)

    yz = yscr[...] * _silu(pc_ref[:, C_XBC:C_XBC + GROUP_W])
    gw = e * C_HEADDIM
    for g in range(C_GROUPS):
        o_ref[:, g * gw:(g + 1) * gw] = _rms(yz[:, g * gw:(g + 1) * gw], ng_ref[:, g * gw:(g + 1) * gw])


def _branch_c(pc, pm, cw, cb, dtb, alog, dsk, ng, l, batch, seq, tc):
    t = pc.shape[0]
    nseq = seq // tc
    hb = tc // C_HALO
    tt = min(tc, 256)
    return pl.pallas_call(
        functools.partial(_c_body, tc=tc),
        grid=(batch, nseq),
        in_specs=[pl.BlockSpec((tc, W_PC), lambda b, j: (b * nseq + j, 0)),
                  pl.BlockSpec((C_HALO, W_PC), lambda b, j: (jnp.maximum((b * nseq + j) * hb - 1, 0), 0)),
                  pl.BlockSpec((tc, W_PM), lambda b, j: (b * nseq + j, 0)),
                  _layer_spec(cw, l), _layer_spec(cb, l), _layer_spec(dtb, l),
                  _layer_spec(alog, l), _layer_spec(dsk, l), _layer_spec(ng, l),
                  pl.BlockSpec((tt, tt), lambda b, j: (0, 0)),
                  pl.BlockSpec((LANES, GROUP_W), lambda b, j: (0, 0))],
        out_specs=pl.BlockSpec((tc, GROUP_W), lambda b, j: (b * nseq + j, 0)),
        out_shape=jax.ShapeDtypeStruct((t, GROUP_W), F32),
        scratch_shapes=[pltpu.VMEM((C_GROUPS * C_STATE, C_HEADS // C_GROUPS * C_HEADDIM), F32),
                        pltpu.VMEM((C_HALO + tc, C_XBC), F32),
                        pltpu.VMEM((tc, C_XBC), F32),
                        pltpu.VMEM((tc, GROUP_W), F32),
                        pltpu.VMEM((tc, GROUP_W), F32),
                        pltpu.VMEM((tc, GROUP_W), F32)],
        compiler_params=_params("arbitrary", "arbitrary"),
        name="branch_c",
    )(pc, pc, pm, cw, cb, dtb, alog, dsk, ng, _group_tril(tt, CHUNK), _head_expand())


def _head_expand():
    r = jnp.arange(LANES)[:, None]
    c = jnp.arange(GROUP_W)[None, :]
    return (r == c // C_HEADDIM).astype(BF16)


HEAD_W = 128
V_ROWS = 80
QK_SCALE = (D_NOPE + D_ROPE) ** -0.5
LOG2E = 1.4426950408889634


def _d_prep_body(pd_ref, pm_ref, posc_ref, posr_ref, qag_ref, wqt_ref, wqrt_ref, kvag_ref, wk_ref, wvt_ref,
                 fr_ref, frc_ref, qt_ref, k_ref, vt_ref):
    sc = QK_SCALE * LOG2E
    hq = _rms(pd_ref[:, 256:512], qag_ref[...]).astype(BF16)
    q0 = lax.dot_general(wqt_ref[...], hq, NT, preferred_element_type=F32)
    q1 = lax.dot_general(wqrt_ref[...], hq, NT, preferred_element_type=F32)
    angt = frc_ref[...] * posr_ref[...]
    cst = jnp.cos(angt)
    snt = jnp.sin(angt)
    for h in range(D_HEADS):
        r0 = h * HEAD_W
        qt_ref[h, 0:D_NOPE, :] = (q0[r0:r0 + D_NOPE] * sc).astype(BF16)
        rope = q0[r0 + D_NOPE:r0 + D_NOPE + D_ROPE] * cst + q1[h * D_ROPE:(h + 1) * D_ROPE] * snt
        qt_ref[h, D_NOPE:D_NOPE + D_ROPE, :] = (rope * sc).astype(BF16)
        qt_ref[h, D_NOPE + D_ROPE:HEAD_W, :] = jnp.zeros((HEAD_W - D_NOPE - D_ROPE, q0.shape[1]), BF16)

    hkv = _rms(pd_ref[:, 512:640], kvag_ref[...]).astype(BF16)
    k0 = jnp.dot(hkv, wk_ref[...], preferred_element_type=F32)
    ang = posc_ref[...] * fr_ref[...]
    cs = jnp.cos(ang)
    sn = jnp.sin(ang)
    lane = lax.broadcasted_iota(jnp.int32, ang.shape, 1)
    rope_lane = (lane >= D_NOPE) & (lane < D_NOPE + D_ROPE)
    kr = jnp.where(rope_lane, pm_ref[:, 0:LANES] * cs, 0.0) + pm_ref[:, LANES:2 * LANES] * sn
    k_ref[...] = (k0 + jnp.concatenate([kr] * D_HEADS, axis=1)).astype(BF16)

    vt = lax.dot_general(wvt_ref[...], hkv, NT, preferred_element_type=F32)
    pad = V_ROWS - D_VDIM
    ones_row = jnp.where(lax.broadcasted_iota(jnp.int32, (pad, vt.shape[1]), 0) == 0, 1.0, 0.0).astype(BF16)
    for h in range(D_HEADS):
        vt_ref[h, 0:D_VDIM, :] = vt[h * D_VDIM:(h + 1) * D_VDIM].astype(BF16)
        vt_ref[h, D_VDIM:V_ROWS, :] = ones_row


def _d_prep(pd, pm, posc, posr, qag, wqt, wqrt, kvag, wk, wvt, fr, frc, l, batch, seq, tm):
    t = pd.shape[0]
    nseq = seq // tm
    return pl.pallas_call(
        _d_prep_body,
        grid=(batch, nseq),
        in_specs=[pl.BlockSpec((tm, W_PD), lambda b, i: (b * nseq + i, 0)),
                  pl.BlockSpec((tm, W_PM), lambda b, i: (b * nseq + i, 0)),
                  pl.BlockSpec((tm, 1), lambda b, i: (b * nseq + i, 0)),
                  pl.BlockSpec((1, tm), lambda b, i: (0, b * nseq + i)),
                  _layer_spec(qag, l), _layer_spec(wqt, l), _layer_spec(wqrt, l),
                  _layer_spec(kvag, l), _layer_spec(wk, l), _layer_spec(wvt, l),
                  pl.BlockSpec(fr.shape, lambda b, i: (0, 0)),
                  pl.BlockSpec(frc.shape, lambda b, i: (0, 0))],
        out_specs=[pl.BlockSpec((None, D_HEADS, HEAD_W, tm), lambda b, i: (b, 0, 0, i)),
                   pl.BlockSpec((tm, D_HEADS * HEAD_W), lambda b, i: (b * nseq + i, 0)),
                   pl.BlockSpec((None, D_HEADS, None, V_ROWS, tm), lambda b, i: (b, 0, i, 0, 0))],
        out_shape=[jax.ShapeDtypeStruct((batch, D_HEADS, HEAD_W, seq), BF16),
                   jax.ShapeDtypeStruct((t, D_HEADS * HEAD_W), BF16),
                   jax.ShapeDtypeStruct((batch, D_HEADS, nseq, V_ROWS, tm), BF16)],
        compiler_params=_params("parallel", "parallel"),
        name="mla_prep",
    )(pd, pm, posc, posr, qag, wqt, wqrt, kvag, wk, wvt, fr, frc)


ATT_HEADS_PER_STEP = 2


def _attn_body(qt_ref, k_ref, vt_ref, o_ref, acc, mrow, sa, sb, *, tq):
    i = pl.program_id(2)
    key_chunk = lax.broadcasted_iota(jnp.int32, (tq, tq), 0) // CHUNK
    qry_chunk = lax.broadcasted_iota(jnp.int32, (tq, tq), 1) // CHUNK
    visible = key_chunk <= qry_chunk
    outs = []
    for hh in range(ATT_HEADS_PER_STEP):
        lanes = slice(hh * HEAD_W, (hh + 1) * HEAD_W)
        qt = qt_ref[hh]

        def scores(kt):
            k0 = pl.multiple_of(kt * tq, tq)
            return jnp.dot(k_ref[pl.ds(k0, tq), lanes], qt, preferred_element_type=F32)

        def fold(st, j):
            m_old = mrow[...]
            m_new = jnp.maximum(m_old, jnp.max(st, axis=0, keepdims=True))
            pt = jnp.exp2(st - m_new).astype(BF16)
            acc[...] = jnp.exp2(m_old - m_new) * acc[...] + jnp.dot(vt_ref[hh, j], pt, preferred_element_type=F32)
            mrow[...] = m_new

        st = jnp.where(visible, scores(i), MASK_VALUE)
        sa[...] = scores(0)
        m0 = jnp.max(st, axis=0, keepdims=True)
        mrow[...] = m0
        acc[...] = jnp.dot(vt_ref[hh, i], jnp.exp2(st - m0).astype(BF16), preferred_element_type=F32)

        def two_tiles(j):
            sb[...] = scores(j + 1)
            fold(sa[...], j)
            sa[...] = scores(j + 2)
            fold(sb[...], j + 1)

        def quad_trip(p, carry):
            two_tiles(4 * p)
            two_tiles(4 * p + 2)
            return carry

        quads = lax.shift_right_logical(jnp.maximum(i - 1, 0), 2)
        lax.fori_loop(0, quads, quad_trip, 0)
        j1 = 4 * quads

        def pair_trip(p, carry):
            two_tiles(j1 + 2 * p)
            return carry

        pairs = lax.shift_right_logical(jnp.maximum(i - 1 - j1, 0), 1)
        lax.fori_loop(0, pairs, pair_trip, 0)
        j0 = j1 + 2 * pairs
        left = i - j0

        @pl.when(left == 2)
        def _():
            sb[...] = scores(j0 + 1)
            fold(sa[...], j0)
            fold(sb[...], j0 + 1)

        @pl.when(left == 1)
        def _():
            fold(sa[...], j0)

        a = acc[...]
        outs.append(a[0:D_VDIM, :] / a[D_VDIM:D_VDIM + 1, :])
    o_ref[...] = jnp.concatenate(outs, axis=0).T


def _attention(qt, k, vt, batch, seq, tq):
    t = k.shape[0]
    nq = seq // tq
    g = ATT_HEADS_PER_STEP
    return pl.pallas_call(
        functools.partial(_attn_body, tq=tq),
        grid=(batch, D_HEADS // g, nq),
        in_specs=[pl.BlockSpec((None, g, HEAD_W, tq), lambda b, h, i: (b, h, 0, i)),
                  pl.BlockSpec((seq, g * HEAD_W), lambda b, h, i: (b, h)),
                  pl.BlockSpec((None, g, nq, V_ROWS, tq), lambda b, h, i: (b, h, 0, 0, 0))],
        out_specs=pl.BlockSpec((tq, g * D_VDIM), lambda b, h, i: (b * nq + i, h)),
        out_shape=jax.ShapeDtypeStruct((t, D_HEADS * D_VDIM), F32),
        scratch_shapes=[pltpu.VMEM((V_ROWS, tq), F32), pltpu.VMEM((1, tq), F32),
                        pltpu.VMEM((tq, tq), F32), pltpu.VMEM((tq, tq), F32)],
        compiler_params=_params("arbitrary", "arbitrary", "arbitrary"),
        name="mla_attention",
    )(qt, k, vt)


def _out_body(ya_ref, yb_ref, yc_ref, od_ref, gd_ref, x_ref, w_ref, pg_ref, o_ref):
    yd = od_ref[...] * _silu(gd_ref[...])
    y = None
    for n, part in enumerate((ya_ref[...], yb_ref[...], yc_ref[...], yd)):
        t = jnp.dot(part.astype(BF16), w_ref[n * GROUP_W:(n + 1) * GROUP_W, :], preferred_element_type=F32)
        y = t if y is None else y + t
    o_ref[...] = x_ref[...] + _rms(y, pg_ref[...])


def _outproj(ya, yb, yc, od, pd, x, w2, post_g, l, tm):
    t = x.shape[0]
    part = pl.BlockSpec((tm, GROUP_W), lambda i: (i, 0))
    return pl.pallas_call(
        _out_body,
        grid=(t // tm,),
        in_specs=[part, part, part, part, part,
                  pl.BlockSpec((tm, D_MODEL), lambda i: (i, 0)),
                  _layer_spec(w2, l), _layer_spec(post_g, l)],
        out_specs=pl.BlockSpec((tm, D_MODEL), lambda i: (i, 0)),
        out_shape=jax.ShapeDtypeStruct((t, D_MODEL), F32),
        compiler_params=_params("parallel"),
        name="outproj",
    )(ya, yb, yc, od, pd, x, w2, post_g)


def _rot_cols(w):
    half = D_ROPE // 2
    return jnp.concatenate([-w[..., half:], w[..., :half]], axis=-1)


def _prep_w_in(w_in):
    d = w_in.shape[0]
    z = lambda n: jnp.zeros((d, D_MODEL, n), w_in.dtype)
    c = lambda a, n: w_in[:, :, a:a + n]
    kr = c(3204, D_ROPE)
    cols = [c(0, 768),
            c(768, 1024),
            c(1792, 768), c(2564, 256),
            c(3236, 256), c(2820, 256), c(3076, 128),
            c(2560, 4), z(60), kr, z(32),
            z(64), _rot_cols(kr), z(32)]
    return jnp.concatenate(cols, axis=-1).astype(BF16)


def _prep_q(qb_w):
    d = qb_w.shape[0]
    z = lambda n: jnp.zeros((d, D_Q_RANK, n), qb_w.dtype)
    plain, rot = [], []
    for h in range(D_HEADS):
        o = h * (D_NOPE + D_ROPE)
        nope, rope = qb_w[:, :, o:o + D_NOPE], qb_w[:, :, o + D_NOPE:o + D_NOPE + D_ROPE]
        plain += [nope, rope, z(HEAD_W - D_NOPE - D_ROPE)]
        rot += [_rot_cols(rope)]
    t = lambda cols: jnp.swapaxes(jnp.concatenate(cols, axis=-1), 1, 2).astype(BF16)
    return t(plain), t(rot)


def _prep_kv(kvb_w):
    d = kvb_w.shape[0]
    z = lambda n: jnp.zeros((d, D_KV_RANK, n), kvb_w.dtype)
    ks, vs = [], []
    for h in range(D_HEADS):
        o = h * (D_NOPE + D_VDIM)
        ks += [kvb_w[:, :, o:o + D_NOPE], z(HEAD_W - D_NOPE)]
        vs += [kvb_w[:, :, o + D_NOPE:o + D_NOPE + D_VDIM]]
    return (jnp.concatenate(ks, axis=-1).astype(BF16),
            jnp.swapaxes(jnp.concatenate(vs, axis=-1), 1, 2).astype(BF16))


def _row(p, width=None):
    if width is not None and width > p.shape[-1]:
        p = jnp.pad(p, ((0, 0), (0, width - p.shape[-1])))
    return p[:, None, :]


def _pick_tile(n, want):
    t = min(want, n)
    while n % t:
        t //= 2
    return t


@jax.jit
def _forward(x, positions, pre_norm_g, post_norm_g, w_in, w_out, a_dw_w, a_dw_b, a_ln_g, a_ln_b,
             a_pw_w, a_pw_b, b_lb_logits, b_norm_g, c_conv_w, c_conv_b, c_dt_bias, c_a_log, c_d,
             c_norm_g, d_qa_g, d_qb_w, d_kva_g, d_kvb_w):
    batch, seq, _ = x.shape
    depth = w_in.shape[0]
    t = batch * seq
    tm = _pick_tile(seq, 512)
    tb = _pick_tile(seq, 256)

    w1 = _prep_w_in(w_in)
    w2 = w_out.astype(BF16)
    wqt, wqrt = _prep_q(d_qb_w)
    wk, wvt = _prep_kv(d_kvb_w)
    dww = jnp.pad(a_dw_w, ((0, 0), (0, 32 - A_KERNEL), (0, 0)))
    pww = a_pw_w.astype(BF16)
    dsk = _row(jnp.repeat(c_d, C_HEADDIM, axis=-1))
    alog_e = _row(jnp.repeat(c_a_log, C_HEADDIM, axis=-1))
    inv_freq = ROPE_THETA ** (-jnp.arange(0, D_ROPE, 2, dtype=F32) / D_ROPE)
    fr = jnp.concatenate([jnp.zeros((D_NOPE,), F32), inv_freq, inv_freq,
                          jnp.zeros((HEAD_W - D_NOPE - D_ROPE,), F32)])[None, :]
    frc = jnp.concatenate([inv_freq, inv_freq])[:, None]
    posc = positions.astype(F32).reshape(t, 1)
    posr = positions.astype(F32).reshape(1, t)

    xf = x.reshape(t, D_MODEL)
    for l in range(depth):
        pa, pb, pc, pd, pm = _inproj(xf, _row(pre_norm_g), w1, l, tm)
        ya = _branch_a(pa, dww, _row(a_dw_b), _row(a_ln_g), _row(a_ln_b), pww, _row(a_pw_b), l, seq, tm)
        yb = _branch_b(pb, b_lb_logits, _row(b_norm_g), l, batch, seq, tb)
        yc = _branch_c(pc, pm, c_conv_w, _row(c_conv_b), _row(c_dt_bias, LANES), alog_e,
                       dsk, _row(c_norm_g), l, batch, seq, tm)
        qt, k, vt = _d_prep(pd, pm, posc, posr, _row(d_qa_g), wqt, wqrt, _row(d_kva_g), wk, wvt, fr, frc,
                            l, batch, seq, tm)
        od = _attention(qt, k, vt, batch, seq, tm)
        xf = _outproj(ya, yb, yc, od, pd, xf, w2, _row(post_norm_g), l, tm)
    return xf.reshape(batch, seq, D_MODEL)


def kernel(x, positions, pre_norm_g, post_norm_g, w_in, w_out, a_dw_w, a_dw_b, a_ln_g, a_ln_b, a_pw_w, a_pw_b, b_lb_logits, b_norm_g, c_conv_w, c_conv_b, c_dt_bias, c_a_log, c_d, c_norm_g, d_qa_g, d_qb_w, d_kva_g, d_kvb_w):
    return _forward(x, positions, pre_norm_g, post_norm_g, w_in, w_out, a_dw_w, a_dw_b, a_ln_g, a_ln_b,
                    a_pw_w, a_pw_b, b_lb_logits, b_norm_g, c_conv_w, c_conv_b, c_dt_bias, c_a_log, c_d,
                    c_norm_g, d_qa_g, d_qb_w, d_kva_g, d_kvb_w)
```

```python
import functools
import math

import jax
import jax.numpy as jnp
from jax import lax
from jax.experimental import pallas as pl
from jax.experimental.pallas import tpu as pltpu

F32 = jnp.float32
BF16 = jnp.bfloat16

D_MODEL = 1024
CHUNK = 64
GROUP_W = 256
NORM_EPS = 1e-6
LN_EPS = 1e-5
MASK_VALUE = -1e30
GATE_FLOOR = 1e-20
A_KERNEL = 31
B_HEADS = 4
B_DK = 64
C_HEADS = 4
C_HEADDIM = 64
C_GROUPS = 2
C_STATE = 128
C_CONV = 4
C_XBC = 768
D_HEADS = 4
D_Q_RANK = 256
D_KV_RANK = 128
D_NOPE = 64
D_ROPE = 32
D_VDIM = 64
ROPE_THETA = 10000.0

LANES = 128
SUB = 16
VMEM_LIMIT = 48 * 1024 * 1024

W_PA, W_PB, W_PC, W_PD, W_PM = 768, 1024, 1024, 640, 256
N_PROJ = W_PA + W_PB + W_PC + W_PD + W_PM

NT = (((1,), (1,)), ((), ()))
TN = (((0,), (0,)), ((), ()))


def _params(*sem):
    return pltpu.CompilerParams(dimension_semantics=sem, vmem_limit_bytes=VMEM_LIMIT)


def _layer_spec(arr, l):
    shape = arr.shape[1:]
    nd = len(shape)
    return pl.BlockSpec((None,) + shape, lambda *_: (l,) + (0,) * nd)


def _split3(x):
    h1 = x.astype(BF16)
    r1 = x - h1.astype(F32)
    h2 = r1.astype(BF16)
    h3 = (r1 - h2.astype(F32)).astype(BF16)
    return h1, h2, h3


def _dot_sel(sel, x, dims=None, sel_on_right=False):
    if dims is None:
        dims = (((1,), (0,)), ((), ()))
    out = None
    for piece in _split3(x):
        a, b = (piece, sel) if sel_on_right else (sel, piece)
        t = lax.dot_general(a, b, dims, preferred_element_type=F32)
        out = t if out is None else out + t
    return out


def _silu(x):
    return x * jax.nn.sigmoid(x)


def _rms(x, g):
    ms = jnp.mean(x * x, axis=-1, keepdims=True)
    return x * lax.rsqrt(ms + NORM_EPS) * g


def _inproj_body(x_ref, g_ref, w_ref, *outs):
    h = _rms(x_ref[...], g_ref[...]).astype(BF16)
    off = 0
    for o in outs:
        n = o.shape[-1]
        o[...] = jnp.dot(h, w_ref[:, off:off + n], preferred_element_type=F32).astype(o.dtype)
        off += n


def _inproj(x, pre_g, w1, l, tm):
    t = x.shape[0]
    widths = (W_PA, W_PB, W_PC, W_PD, W_PM)
    return pl.pallas_call(
        _inproj_body,
        grid=(t // tm,),
        in_specs=[pl.BlockSpec((tm, D_MODEL), lambda i: (i, 0)),
                  _layer_spec(pre_g, l), _layer_spec(w1, l)],
        out_specs=[pl.BlockSpec((tm, n), lambda i: (i, 0)) for n in widths],
        out_shape=[jax.ShapeDtypeStruct((t, n), F32) for n in widths],
        compiler_params=_params("parallel"),
        name="inproj",
    )(x, pre_g, w1)


A_HALO = 32


def _a_body(pa_ref, ph_ref, dww_ref, dwb_ref, lng_ref, lnb_ref, pww_ref, pwb_ref, o_ref,
            hscr, cscr, *, ts, nseq):
    first = (pl.program_id(0) % nseq) == 0
    h = pa_ref[:, 0:256] * jax.nn.sigmoid(pa_ref[:, 256:512])
    hh = ph_ref[:, 0:256] * jax.nn.sigmoid(ph_ref[:, 256:512])
    hscr[0:A_HALO, :] = jnp.where(first, 0.0, hh)
    hscr[A_HALO:A_HALO + ts, :] = h
    w = dww_ref[...]
    rows = 64
    base = A_HALO - (A_KERNEL - 1)
    for r in range(ts // rows):
        acc = jnp.broadcast_to(dwb_ref[...], (rows, GROUP_W))
        for j in range(A_KERNEL):
            s0 = r * rows + base + j
            acc = acc + w[j:j + 1, :] * hscr[s0:s0 + rows, :]
        cscr[r * rows:(r + 1) * rows, :] = acc
    c = cscr[...]
    mu = jnp.mean(c, axis=-1, keepdims=True)
    d = c - mu
    var = jnp.mean(d * d, axis=-1, keepdims=True)
    y = d * lax.rsqrt(var + LN_EPS) * lng_ref[...] + lnb_ref[...]
    y = _silu(y).astype(BF16)
    y = jnp.dot(y, pww_ref[...], preferred_element_type=F32) + pwb_ref[...]
    o_ref[...] = y * _silu(pa_ref[:, 512:768])


def _branch_a(pa, dww, dwb, lng, lnb, pww, pwb, l, seq, ts):
    t = pa.shape[0]
    nseq = seq // ts
    hb = ts // A_HALO
    return pl.pallas_call(
        functools.partial(_a_body, ts=ts, nseq=nseq),
        grid=(t // ts,),
        in_specs=[pl.BlockSpec((ts, W_PA), lambda i: (i, 0)),
                  pl.BlockSpec((A_HALO, W_PA), lambda i: (jnp.maximum(i * hb - 1, 0), 0)),
                  _layer_spec(dww, l), _layer_spec(dwb, l), _layer_spec(lng, l),
                  _layer_spec(lnb, l), _layer_spec(pww, l), _layer_spec(pwb, l)],
        out_specs=pl.BlockSpec((ts, GROUP_W), lambda i: (i, 0)),
        out_shape=jax.ShapeDtypeStruct((t, GROUP_W), F32),
        scratch_shapes=[pltpu.VMEM((A_HALO + ts, GROUP_W), F32),
                        pltpu.VMEM((ts, GROUP_W), F32)],
        compiler_params=_params("arbitrary"),
        name="branch_a",
    )(pa, pa, dww, dwb, lng, lnb, pww, pwb)


def _group_tril(n, group):
    r = jnp.arange(n)[:, None]
    c = jnp.arange(n)[None, :]
    return ((r // group == c // group) & (c <= r)).astype(BF16)


def _b_body(pb_ref, lbl_ref, ng_ref, tril_ref, o_ref, st_ref, kscr, bscr, oscr, *, l, tb):
    @pl.when(pl.program_id(1) == 0)
    def _():
        st_ref[...] = jnp.zeros_like(st_ref)

    lg = lbl_ref[...]
    e = jnp.exp(lg - jnp.max(lg, axis=0, keepdims=True))
    p = e / jnp.sum(e, axis=0, keepdims=True)
    lb = jnp.zeros((1, GROUP_W), F32)
    for m in range(1, l + 1):
        lb = lb + p[m:m + 1, :]

    zf = pb_ref[:, 256:512]
    fg = lb + (1.0 - lb) * jax.nn.sigmoid(zf)
    logf = jnp.log(jnp.maximum(fg, GATE_FLOOR))
    kscr[...] = (1.0 - lb) * jax.nn.sigmoid(-zf)
    bscr[...] = _dot_sel(tril_ref[...], logf) * LOG2E

    hr = lax.broadcasted_iota(jnp.int32, (GROUP_W, GROUP_W), 0) // B_DK
    hc = lax.broadcasted_iota(jnp.int32, (GROUP_W, GROUP_W), 1) // B_DK
    same_head = hr == hc
    bd = jnp.where(same_head, 1.0, 0.0).astype(BF16)
    half = SUB // 2
    tio = lax.broadcasted_iota(jnp.int32, (SUB, GROUP_W), 0)
    tio_hi = lax.broadcasted_iota(jnp.int32, (half, GROUP_W), 0) + half
    zero_lo = jnp.zeros((half, GROUP_W), F32)
    head_rows = (lax.broadcasted_iota(jnp.int32, (B_HEADS * SUB, GROUP_W), 0) // SUB
                 == lax.broadcasted_iota(jnp.int32, (B_HEADS * SUB, GROUP_W), 1) // B_DK)

    def intra(qj, kj, vj, bj):
        ws = []
        for s in range(SUB):
            if s < half:
                w = jnp.where(tio >= s, qj * kj[s:s + 1, :] * jnp.exp2(bj - bj[s:s + 1, :]), 0.0)
            else:
                w = jnp.where(tio_hi >= s, qj[half:] * kj[s:s + 1, :] * jnp.exp2(bj[half:] - bj[s:s + 1, :]), 0.0)
                w = jnp.concatenate([zero_lo, w], axis=0)
            ws.append(w.astype(BF16))
        m = jnp.dot(jnp.concatenate(ws, axis=0), bd, preferred_element_type=F32)
        o_lo = jnp.zeros((half, GROUP_W), F32)
        o_hi = jnp.zeros((half, GROUP_W), F32)
        for s in range(SUB):
            vrow = vj[s:s + 1, :]
            if s < half:
                o_lo = o_lo + m[s * SUB:s * SUB + half, :] * vrow
            o_hi = o_hi + m[s * SUB + half:(s + 1) * SUB, :] * vrow
        return jnp.concatenate([o_lo, o_hi], axis=0)

    def state_step(st, qj, kj, vj, bj):
        qt = (qj * jnp.exp2(bj)).astype(BF16)
        o = lax.dot_general(qt, st.astype(BF16), NT, preferred_element_type=F32)
        blast = bj[SUB - 1:SUB, :]
        kh = (kj * jnp.exp2(blast - bj)).astype(BF16)
        u = lax.dot_general(vj.astype(BF16), kh, TN, preferred_element_type=F32)
        return o, st * jnp.exp2(blast) + jnp.where(same_head, u, 0.0)

    group = 8

    def trip(jj, carry):
        r0 = pl.multiple_of(jj * (group * SUB), group * SUB)
        rows = pl.ds(r0, group * SUB)
        qg, vg = pb_ref[rows, 0:256], pb_ref[rows, 512:768]
        kg, bg = kscr[rows, :], bscr[rows, :]
        st = st_ref[...]
        parts = [slice(n * SUB, (n + 1) * SUB) for n in range(group)]
        outs = [intra(qg[p], kg[p], vg[p], bg[p]) for p in parts]
        for n, p in enumerate(parts):
            o_inter, st = state_step(st, qg[p], kg[p], vg[p], bg[p])
            outs[n] = outs[n] + o_inter
        oscr[rows, :] = jnp.concatenate(outs, axis=0)
        st_ref[...] = st
        return carry

    lax.fori_loop(0, tb // (group * SUB), trip, 0)

    o = oscr[...]
    ms = _dot_sel(bd, o * o, sel_on_right=True) * (1.0 / B_DK)
    y = o * lax.rsqrt(ms + NORM_EPS) * ng_ref[...]
    o_ref[...] = y * _silu(pb_ref[:, 768:1024])


def _branch_b(pb, lbl, ng, l, batch, seq, tb):
    t = pb.shape[0]
    nseq = seq // tb
    return pl.pallas_call(
        functools.partial(_b_body, l=l, tb=tb),
        grid=(batch, nseq),
        in_specs=[pl.BlockSpec((tb, W_PB), lambda b, j: (b * nseq + j, 0)),
                  pl.BlockSpec(lbl.shape, lambda b, j: (0, 0)),
                  _layer_spec(ng, l),
                  pl.BlockSpec((tb, tb), lambda b, j: (0, 0))],
        out_specs=pl.BlockSpec((tb, GROUP_W), lambda b, j: (b * nseq + j, 0)),
        out_shape=jax.ShapeDtypeStruct((t, GROUP_W), F32),
        scratch_shapes=[pltpu.VMEM((GROUP_W, GROUP_W), F32),
                        pltpu.VMEM((tb, GROUP_W), F32),
                        pltpu.VMEM((tb, GROUP_W), F32),
                        pltpu.VMEM((tb, GROUP_W), F32)],
        compiler_params=_params("arbitrary", "arbitrary"),
        name="branch_b",
    )(pb, lbl, ng, _group_tril(tb, SUB))


C_HALO = 8


def _c_body(pc_ref, ph_ref, pm_ref, cw_ref, cb_ref, dtb_ref, alog_ref, dsk_ref, ng_ref, tril_ref, exp_ref,
            o_ref, h_ref, xscr, cscr, aescr, xdscr, yscr, *, tc):
    first = pl.program_id(1) == 0

    @pl.when(first)
    def _():
        h_ref[...] = jnp.zeros_like(h_ref)

    xscr[0:C_HALO, :] = jnp.where(first, 0.0, ph_ref[:, 0:C_XBC])
    xscr[C_HALO:C_HALO + tc, :] = pc_ref[:, 0:C_XBC]
    rows = 64
    base = C_HALO - (C_CONV - 1)
    for lb in range(C_XBC // GROUP_W):
        ls = slice(lb * GROUP_W, (lb + 1) * GROUP_W)
        w = cw_ref[:, ls]
        bias = jnp.broadcast_to(cb_ref[:, ls], (rows, GROUP_W))
        for r in range(tc // rows):
            acc = bias
            for j in range(C_CONV):
                s0 = r * rows + base + j
                acc = acc + w[j:j + 1, :] * xscr[s0:s0 + rows, ls]
            cscr[r * rows:(r + 1) * rows, ls] = _silu(acc)

    dt = jnp.logaddexp(pm_ref[:, 0:LANES] + dtb_ref[...], 0.0)
    dte = _dot_sel(exp_ref[...], dt, sel_on_right=True)
    xdscr[...] = cscr[:, 0:GROUP_W] * dte
    a = dte * (-jnp.exp(alog_ref[...]))
    tb = tril_ref.shape[0]
    for r in range(tc // tb):
        aescr[r * tb:(r + 1) * tb, :] = _dot_sel(tril_ref[...], a[r * tb:(r + 1) * tb, :])

    e = C_HEADS // C_GROUPS
    gw = e * C_HEADDIM
    sel = jnp.where(lax.broadcasted_iota(jnp.int32, (8, GROUP_W), 0) * C_HEADDIM
                    == lax.broadcasted_iota(jnp.int32, (8, GROUP_W), 1), 1.0, 0.0).astype(BF16)
    li = lax.broadcasted_iota(jnp.int32, (CHUNK, gw), 0)
    lane = lax.broadcasted_iota(jnp.int32, (CHUNK, gw), 1)
    causal = li >= lane % C_HEADDIM
    first_head = lane < C_HEADDIM
    dsk = dsk_ref[...]

    def chunk(c, carry):
        r0 = pl.multiple_of(c * CHUNK, CHUNK)
        rs = pl.ds(r0, CHUNK)
        ae = aescr[rs, :]
        alast = ae[CHUNK - 1:CHUNK, :]
        arow = _dot_sel(sel, ae, NT)
        xs = cscr[rs, 0:GROUP_W]
        xdt = xdscr[rs, :]
        wx = xdt * jnp.exp(alast - ae)
        for g in range(C_GROUPS):
            ls = slice(g * gw, (g + 1) * gw)
            bmg = cscr[rs, GROUP_W + g * C_STATE:GROUP_W + (g + 1) * C_STATE].astype(BF16)
            c0 = GROUP_W + C_GROUPS * C_STATE + g * C_STATE
            cmg = cscr[rs, c0:c0 + C_STATE].astype(BF16)
            cb2 = lax.dot_general(cmg, jnp.concatenate([bmg, bmg], axis=0), NT, preferred_element_type=F32)
            rowpair = jnp.concatenate([arow[g * e:g * e + 1, :], arow[g * e + 1:g * e + 2, :]], axis=1)
            lm2 = jnp.where(causal, jnp.exp(jnp.minimum(ae[:, ls] - rowpair, 0.0)), 0.0)
            xg = xdt[:, ls]
            x2 = jnp.concatenate([jnp.where(first_head, xg, 0.0), jnp.where(first_head, 0.0, xg)], axis=0)
            ydiag = jnp.dot((cb2 * lm2).astype(BF16), x2.astype(BF16), preferred_element_type=F32)
            ht = h_ref[g * C_STATE:(g + 1) * C_STATE, :]
            yoff = jnp.dot(cmg, ht.astype(BF16), preferred_element_type=F32) * jnp.exp(ae[:, ls])
            stt = lax.dot_general(bmg, wx[:, ls].astype(BF16), TN, preferred_element_type=F32)
            h_ref[g * C_STATE:(g + 1) * C_STATE, :] = ht * jnp.exp(alast[:, ls]) + stt
            yscr[rs, ls] = ydiag + yoff + xs[:, ls] * dsk[:, ls]
        return carry

    lax.fori_loop(0, tc // CHUNK, chunk, 0, unroll=2)

    yz = yscr[...] * _silu(pc_ref[:, C_XBC:C_XBC + GROUP_W])
    gw = e * C_HEADDIM
    for g in range(C_GROUPS):
        o_ref[:, g * gw:(g + 1) * gw] = _rms(yz[:, g * gw:(g + 1) * gw], ng_ref[:, g * gw:(g + 1) * gw])


def _branch_c(pc, pm, cw, cb, dtb, alog, dsk, ng, l, batch, seq, tc):
    t = pc.shape[0]
    nseq = seq // tc
    hb = tc // C_HALO
    tt = min(tc, 256)
    return pl.pallas_call(
        functools.partial(_c_body, tc=tc),
        grid=(batch, nseq),
        in_specs=[pl.BlockSpec((tc, W_PC), lambda b, j: (b * nseq + j, 0)),
                  pl.BlockSpec((C_HALO, W_PC), lambda b, j: (jnp.maximum((b * nseq + j) * hb - 1, 0), 0)),
                  pl.BlockSpec((tc, W_PM), lambda b, j: (b * nseq + j, 0)),
                  _layer_spec(cw, l), _layer_spec(cb, l), _layer_spec(dtb, l),
                  _layer_spec(alog, l), _layer_spec(dsk, l), _layer_spec(ng, l),
                  pl.BlockSpec((tt, tt), lambda b, j: (0, 0)),
                  pl.BlockSpec((LANES, GROUP_W), lambda b, j: (0, 0))],
        out_specs=pl.BlockSpec((tc, GROUP_W), lambda b, j: (b * nseq + j, 0)),
        out_shape=jax.ShapeDtypeStruct((t, GROUP_W), F32),
        scratch_shapes=[pltpu.VMEM((C_GROUPS * C_STATE, C_HEADS // C_GROUPS * C_HEADDIM), F32),
                        pltpu.VMEM((C_HALO + tc, C_XBC), F32),
                        pltpu.VMEM((tc, C_XBC), F32),
                        pltpu.VMEM((tc, GROUP_W), F32),
                        pltpu.VMEM((tc, GROUP_W), F32),
                        pltpu.VMEM((tc, GROUP_W), F32)],
        compiler_params=_params("arbitrary", "arbitrary"),
        name="branch_c",
    )(pc, pc, pm, cw, cb, dtb, alog, dsk, ng, _group_tril(tt, CHUNK), _head_expand())


def _head_expand():
    r = jnp.arange(LANES)[:, None]
    c = jnp.arange(GROUP_W)[None, :]
    return (r == c // C_HEADDIM).astype(BF16)


HEAD_W = 128
V_ROWS = 80
QK_SCALE = (D_NOPE + D_ROPE) ** -0.5
LOG2E = 1.4426950408889634


def _d_prep_body(pd_ref, pm_ref, posc_ref, posr_ref, qag_ref, wqt_ref, wqrt_ref, kvag_ref, wk_ref, wvt_ref,
                 fr_ref, frc_ref, qt_ref, k_ref, vt_ref):
    sc = QK_SCALE * LOG2E
    hq = _rms(pd_ref[:, 256:512], qag_ref[...]).astype(BF16)
    q0 = lax.dot_general(wqt_ref[...], hq, NT, preferred_element_type=F32)
    q1 = lax.dot_general(wqrt_ref[...], hq, NT, preferred_element_type=F32)
    angt = frc_ref[...] * posr_ref[...]
    cst = jnp.cos(angt)
    snt = jnp.sin(angt)
    for h in range(D_HEADS):
        r0 = h * HEAD_W
        qt_ref[h, 0:D_NOPE, :] = (q0[r0:r0 + D_NOPE] * sc).astype(BF16)
        rope = q0[r0 + D_NOPE:r0 + D_NOPE + D_ROPE] * cst + q1[h * D_ROPE:(h + 1) * D_ROPE] * snt
        qt_ref[h, D_NOPE:D_NOPE + D_ROPE, :] = (rope * sc).astype(BF16)
        qt_ref[h, D_NOPE + D_ROPE:HEAD_W, :] = jnp.zeros((HEAD_W - D_NOPE - D_ROPE, q0.shape[1]), BF16)

    hkv = _rms(pd_ref[:, 512:640], kvag_ref[...]).astype(BF16)
    k0 = jnp.dot(hkv, wk_ref[...], preferred_element_type=F32)
    ang = posc_ref[...] * fr_ref[...]
    cs = jnp.cos(ang)
    sn = jnp.sin(ang)
    lane = lax.broadcasted_iota(jnp.int32, ang.shape, 1)
    rope_lane = (lane >= D_NOPE) & (lane < D_NOPE + D_ROPE)
    kr = jnp.where(rope_lane, pm_ref[:, 0:LANES] * cs, 0.0) + pm_ref[:, LANES:2 * LANES] * sn
    k_ref[...] = (k0 + jnp.concatenate([kr] * D_HEADS, axis=1)).astype(BF16)

    vt = lax.dot_general(wvt_ref[...], hkv, NT, preferred_element_type=F32)
    pad = V_ROWS - D_VDIM
    ones_row = jnp.where(lax.broadcasted_iota(jnp.int32, (pad, vt.shape[1]), 0) == 0, 1.0, 0.0).astype(BF16)
    for h in range(D_HEADS):
        vt_ref[h, 0:D_VDIM, :] = vt[h * D_VDIM:(h + 1) * D_VDIM].astype(BF16)
        vt_ref[h, D_VDIM:V_ROWS, :] = ones_row


def _d_prep(pd, pm, posc, posr, qag, wqt, wqrt, kvag, wk, wvt, fr, frc, l, batch, seq, tm):
    t = pd.shape[0]
    nseq = seq // tm
    return pl.pallas_call(
        _d_prep_body,
        grid=(batch, nseq),
        in_specs=[pl.BlockSpec((tm, W_PD), lambda b, i: (b * nseq + i, 0)),
                  pl.BlockSpec((tm, W_PM), lambda b, i: (b * nseq + i, 0)),
                  pl.BlockSpec((tm, 1), lambda b, i: (b * nseq + i, 0)),
                  pl.BlockSpec((1, tm), lambda b, i: (0, b * nseq + i)),
                  _layer_spec(qag, l), _layer_spec(wqt, l), _layer_spec(wqrt, l),
                  _layer_spec(kvag, l), _layer_spec(wk, l), _layer_spec(wvt, l),
                  pl.BlockSpec(fr.shape, lambda b, i: (0, 0)),
                  pl.BlockSpec(frc.shape, lambda b, i: (0, 0))],
        out_specs=[pl.BlockSpec((None, D_HEADS, HEAD_W, tm), lambda b, i: (b, 0, 0, i)),
                   pl.BlockSpec((tm, D_HEADS * HEAD_W), lambda b, i: (b * nseq + i, 0)),
                   pl.BlockSpec((None, D_HEADS, None, V_ROWS, tm), lambda b, i: (b, 0, i, 0, 0))],
        out_shape=[jax.ShapeDtypeStruct((batch, D_HEADS, HEAD_W, seq), BF16),
                   jax.ShapeDtypeStruct((t, D_HEADS * HEAD_W), BF16),
                   jax.ShapeDtypeStruct((batch, D_HEADS, nseq, V_ROWS, tm), BF16)],
        compiler_params=_params("parallel", "parallel"),
        name="mla_prep",
    )(pd, pm, posc, posr, qag, wqt, wqrt, kvag, wk, wvt, fr, frc)


ATT_HEADS_PER_STEP = 2


def _attn_body(qt_ref, k_ref, vt_ref, o_ref, acc, mrow, sa, sb, *, tq):
    i = pl.program_id(2)
    key_chunk = lax.broadcasted_iota(jnp.int32, (tq, tq), 0) // CHUNK
    qry_chunk = lax.broadcasted_iota(jnp.int32, (tq, tq), 1) // CHUNK
    visible = key_chunk <= qry_chunk
    heads = range(ATT_HEADS_PER_STEP)

    def scores(hh, kt):
        k0 = pl.multiple_of(kt * tq, tq)
        return jnp.dot(k_ref[pl.ds(k0, tq), hh * HEAD_W:(hh + 1) * HEAD_W], qt_ref[hh],
                       preferred_element_type=F32)

    def fold(hh, st, j):
        m_old = mrow[hh]
        m_new = jnp.maximum(m_old, jnp.max(st, axis=0, keepdims=True))
        pt = jnp.exp2(st - m_new).astype(BF16)
        acc[hh] = jnp.exp2(m_old - m_new) * acc[hh] + jnp.dot(vt_ref[hh, j], pt, preferred_element_type=F32)
        mrow[hh] = m_new

    for hh in heads:
        st = jnp.where(visible, scores(hh, i), MASK_VALUE)
        sa[hh] = scores(hh, 0)
        m0 = jnp.max(st, axis=0, keepdims=True)
        mrow[hh] = m0
        acc[hh] = jnp.dot(vt_ref[hh, i], jnp.exp2(st - m0).astype(BF16), preferred_element_type=F32)

    def two_tiles(j):
        for hh in heads:
            sb[hh] = scores(hh, j + 1)
            fold(hh, sa[hh], j)
        for hh in heads:
            sa[hh] = scores(hh, j + 2)
            fold(hh, sb[hh], j + 1)

    def quad_trip(p, carry):
        two_tiles(4 * p)
        two_tiles(4 * p + 2)
        return carry

    quads = lax.shift_right_logical(jnp.maximum(i - 1, 0), 2)
    lax.fori_loop(0, quads, quad_trip, 0)
    j1 = 4 * quads

    def pair_trip(p, carry):
        two_tiles(j1 + 2 * p)
        return carry

    pairs = lax.shift_right_logical(jnp.maximum(i - 1 - j1, 0), 1)
    lax.fori_loop(0, pairs, pair_trip, 0)
    j0 = j1 + 2 * pairs
    left = i - j0

    @pl.when(left == 2)
    def _():
        for hh in heads:
            sb[hh] = scores(hh, j0 + 1)
            fold(hh, sa[hh], j0)
        for hh in heads:
            fold(hh, sb[hh], j0 + 1)

    @pl.when(left == 1)
    def _():
        for hh in heads:
            fold(hh, sa[hh], j0)

    outs = []
    for hh in heads:
        a = acc[hh]
        outs.append(a[0:D_VDIM, :] / a[D_VDIM:D_VDIM + 1, :])
    o_ref[...] = jnp.concatenate(outs, axis=0).T


def _attention(qt, k, vt, batch, seq, tq):
    t = k.shape[0]
    nq = seq // tq
    g = ATT_HEADS_PER_STEP
    return pl.pallas_call(
        functools.partial(_attn_body, tq=tq),
        grid=(batch, D_HEADS // g, nq),
        in_specs=[pl.BlockSpec((None, g, HEAD_W, tq), lambda b, h, i: (b, h, 0, i)),
                  pl.BlockSpec((seq, g * HEAD_W), lambda b, h, i: (b, h)),
                  pl.BlockSpec((None, g, nq, V_ROWS, tq), lambda b, h, i: (b, h, 0, 0, 0))],
        out_specs=pl.BlockSpec((tq, g * D_VDIM), lambda b, h, i: (b * nq + i, h)),
        out_shape=jax.ShapeDtypeStruct((t, D_HEADS * D_VDIM), F32),
        scratch_shapes=[pltpu.VMEM((g, V_ROWS, tq), F32), pltpu.VMEM((g, 1, tq), F32),
                        pltpu.VMEM((g, tq, tq), F32), pltpu.VMEM((g, tq, tq), F32)],
        compiler_params=_params("arbitrary", "arbitrary", "arbitrary"),
        name="mla_attention",
    )(qt, k, vt)


def _out_body(ya_ref, yb_ref, yc_ref, od_ref, gd_ref, x_ref, w_ref, pg_ref, o_ref):
    yd = od_ref[...] * _silu(gd_ref[...])
    y = None
    for n, part in enumerate((ya_ref[...], yb_ref[...], yc_ref[...], yd)):
        t = jnp.dot(part.astype(BF16), w_ref[n * GROUP_W:(n + 1) * GROUP_W, :], preferred_element_type=F32)
        y = t if y is None else y + t
    o_ref[...] = x_ref[...] + _rms(y, pg_ref[...])


def _outproj(ya, yb, yc, od, pd, x, w2, post_g, l, tm):
    t = x.shape[0]
    part = pl.BlockSpec((tm, GROUP_W), lambda i: (i, 0))
    return pl.pallas_call(
        _out_body,
        grid=(t // tm,),
        in_specs=[part, part, part, part, part,
                  pl.BlockSpec((tm, D_MODEL), lambda i: (i, 0)),
                  _layer_spec(w2, l), _layer_spec(post_g, l)],
        out_specs=pl.BlockSpec((tm, D_MODEL), lambda i: (i, 0)),
        out_shape=jax.ShapeDtypeStruct((t, D_MODEL), F32),
        compiler_params=_params("parallel"),
        name="outproj",
    )(ya, yb, yc, od, pd, x, w2, post_g)


def _rot_cols(w):
    half = D_ROPE // 2
    return jnp.concatenate([-w[..., half:], w[..., :half]], axis=-1)


def _prep_w_in(w_in):
    d = w_in.shape[0]
    z = lambda n: jnp.zeros((d, D_MODEL, n), w_in.dtype)
    c = lambda a, n: w_in[:, :, a:a + n]
    kr = c(3204, D_ROPE)
    cols = [c(0, 768),
            c(768, 1024),
            c(1792, 768), c(2564, 256),
            c(3236, 256), c(2820, 256), c(3076, 128),
            c(2560, 4), z(60), kr, z(32),
            z(64), _rot_cols(kr), z(32)]
    return jnp.concatenate(cols, axis=-1).astype(BF16)


def _prep_q(qb_w):
    d = qb_w.shape[0]
    z = lambda n: jnp.zeros((d, D_Q_RANK, n), qb_w.dtype)
    plain, rot = [], []
    for h in range(D_HEADS):
        o = h * (D_NOPE + D_ROPE)
        nope, rope = qb_w[:, :, o:o + D_NOPE], qb_w[:, :, o + D_NOPE:o + D_NOPE + D_ROPE]
        plain += [nope, rope, z(HEAD_W - D_NOPE - D_ROPE)]
        rot += [_rot_cols(rope)]
    t = lambda cols: jnp.swapaxes(jnp.concatenate(cols, axis=-1), 1, 2).astype(BF16)
    return t(plain), t(rot)


def _prep_kv(kvb_w):
    d = kvb_w.shape[0]
    z = lambda n: jnp.zeros((d, D_KV_RANK, n), kvb_w.dtype)
    ks, vs = [], []
    for h in range(D_HEADS):
        o = h * (D_NOPE + D_VDIM)
        ks += [kvb_w[:, :, o:o + D_NOPE], z(HEAD_W - D_NOPE)]
        vs += [kvb_w[:, :, o + D_NOPE:o + D_NOPE + D_VDIM]]
    return (jnp.concatenate(ks, axis=-1).astype(BF16),
            jnp.swapaxes(jnp.concatenate(vs, axis=-1), 1, 2).astype(BF16))


def _row(p, width=None):
    if width is not None and width > p.shape[-1]:
        p = jnp.pad(p, ((0, 0), (0, width - p.shape[-1])))
    return p[:, None, :]


def _pick_tile(n, want):
    t = min(want, n)
    while n % t:
        t //= 2
    return t


@jax.jit
def _forward(x, positions, pre_norm_g, post_norm_g, w_in, w_out, a_dw_w, a_dw_b, a_ln_g, a_ln_b,
             a_pw_w, a_pw_b, b_lb_logits, b_norm_g, c_conv_w, c_conv_b, c_dt_bias, c_a_log, c_d,
             c_norm_g, d_qa_g, d_qb_w, d_kva_g, d_kvb_w):
    batch, seq, _ = x.shape
    depth = w_in.shape[0]
    t = batch * seq
    tm = _pick_tile(seq, 512)
    tb = _pick_tile(seq, 256)

    w1 = _prep_w_in(w_in)
    w2 = w_out.astype(BF16)
    wqt, wqrt = _prep_q(d_qb_w)
    wk, wvt = _prep_kv(d_kvb_w)
    dww = jnp.pad(a_dw_w, ((0, 0), (0, 32 - A_KERNEL), (0, 0)))
    pww = a_pw_w.astype(BF16)
    dsk = _row(jnp.repeat(c_d, C_HEADDIM, axis=-1))
    alog_e = _row(jnp.repeat(c_a_log, C_HEADDIM, axis=-1))
    inv_freq = ROPE_THETA ** (-jnp.arange(0, D_ROPE, 2, dtype=F32) / D_ROPE)
    fr = jnp.concatenate([jnp.zeros((D_NOPE,), F32), inv_freq, inv_freq,
                          jnp.zeros((HEAD_W - D_NOPE - D_ROPE,), F32)])[None, :]
    frc = jnp.concatenate([inv_freq, inv_freq])[:, None]
    posc = positions.astype(F32).reshape(t, 1)
    posr = positions.astype(F32).reshape(1, t)

    xf = x.reshape(t, D_MODEL)
    for l in range(depth):
        pa, pb, pc, pd, pm = _inproj(xf, _row(pre_norm_g), w1, l, tm)
        ya = _branch_a(pa, dww, _row(a_dw_b), _row(a_ln_g), _row(a_ln_b), pww, _row(a_pw_b), l, seq, tm)
        yb = _branch_b(pb, b_lb_logits, _row(b_norm_g), l, batch, seq, tb)
        yc = _branch_c(pc, pm, c_conv_w, _row(c_conv_b), _row(c_dt_bias, LANES), alog_e,
                       dsk, _row(c_norm_g), l, batch, seq, tm)
        qt, k, vt = _d_prep(pd, pm, posc, posr, _row(d_qa_g), wqt, wqrt, _row(d_kva_g), wk, wvt, fr, frc,
                            l, batch, seq, tm)
        od = _attention(qt, k, vt, batch, seq, tm)
        xf = _outproj(ya, yb, yc, od, pd, xf, w2, _row(post_norm_g), l, tm)
    return xf.reshape(batch, seq, D_MODEL)


def kernel(x, positions, pre_norm_g, post_norm_g, w_in, w_out, a_dw_w, a_dw_b, a_ln_g, a_ln_b, a_pw_w, a_pw_b, b_lb_logits, b_norm_g, c_conv_w, c_conv_b, c_dt_bias, c_a_log, c_d, c_norm_g, d_qa_g, d_qb_w, d_kva_g, d_kvb_w):
    return _forward(x, positions, pre_norm_g, post_norm_g, w_in, w_out, a_dw_w, a_dw_b, a_ln_g, a_ln_b,
                    a_pw_w, a_pw_b, b_lb_logits, b_norm_g, c_conv_w, c_conv_b, c_dt_bias, c_a_log, c_d,
                    c_norm_g, d_qa_g, d_qb_w, d_kva_g, d_kvb_w)
```

```python
import functools
import math

import jax
import jax.numpy as jnp
from jax import lax
from jax.experimental import pallas as pl
from jax.experimental.pallas import tpu as pltpu

F32 = jnp.float32
BF16 = jnp.bfloat16

D_MODEL = 1024
CHUNK = 64
GROUP_W = 256
NORM_EPS = 1e-6
LN_EPS = 1e-5
MASK_VALUE = -1e30
GATE_FLOOR = 1e-20
A_KERNEL = 31
B_HEADS = 4
B_DK = 64
C_HEADS = 4
C_HEADDIM = 64
C_GROUPS = 2
C_STATE = 128
C_CONV = 4
C_XBC = 768
D_HEADS = 4
D_Q_RANK = 256
D_KV_RANK = 128
D_NOPE = 64
D_ROPE = 32
D_VDIM = 64
ROPE_THETA = 10000.0

LANES = 128
SUB = 16
VMEM_LIMIT = 48 * 1024 * 1024

W_PA, W_PB, W_PC, W_PD, W_PM = 768, 1024, 1024, 640, 256
N_PROJ = W_PA + W_PB + W_PC + W_PD + W_PM

NT = (((1,), (1,)), ((), ()))
TN = (((0,), (0,)), ((), ()))


def _params(*sem):
    return pltpu.CompilerParams(dimension_semantics=sem, vmem_limit_bytes=VMEM_LIMIT)


def _layer_spec(arr, l):
    shape = arr.shape[1:]
    nd = len(shape)
    return pl.BlockSpec((None,) + shape, lambda *_: (l,) + (0,) * nd)


def _split3(x):
    h1 = x.astype(BF16)
    r1 = x - h1.astype(F32)
    h2 = r1.astype(BF16)
    h3 = (r1 - h2.astype(F32)).astype(BF16)
    return h1, h2, h3


def _dot_sel(sel, x, dims=None, sel_on_right=False):
    if dims is None:
        dims = (((1,), (0,)), ((), ()))
    out = None
    for piece in _split3(x):
        a, b = (piece, sel) if sel_on_right else (sel, piece)
        t = lax.dot_general(a, b, dims, preferred_element_type=F32)
        out = t if out is None else out + t
    return out


def _silu(x):
    return x * jax.nn.sigmoid(x)


def _rms(x, g):
    ms = jnp.mean(x * x, axis=-1, keepdims=True)
    return x * lax.rsqrt(ms + NORM_EPS) * g


def _inproj_body(x_ref, g_ref, w_ref, *outs):
    h = _rms(x_ref[...], g_ref[...]).astype(BF16)
    off = 0
    for o in outs:
        n = o.shape[-1]
        o[...] = jnp.dot(h, w_ref[:, off:off + n], preferred_element_type=F32).astype(o.dtype)
        off += n


def _inproj(x, pre_g, w1, l, tm):
    t = x.shape[0]
    widths = (W_PA, W_PB, W_PC, W_PD, W_PM)
    return pl.pallas_call(
        _inproj_body,
        grid=(t // tm,),
        in_specs=[pl.BlockSpec((tm, D_MODEL), lambda i: (i, 0)),
                  _layer_spec(pre_g, l), _layer_spec(w1, l)],
        out_specs=[pl.BlockSpec((tm, n), lambda i: (i, 0)) for n in widths],
        out_shape=[jax.ShapeDtypeStruct((t, n), F32) for n in widths],
        compiler_params=_params("parallel"),
        name="inproj",
    )(x, pre_g, w1)


A_HALO = 32


SUBLANES = 8


def _a_body(pa_ref, ph_ref, dww_ref, dwb_ref, lng_ref, lnb_ref, pww_ref, pwb_ref, o_ref,
            hscr, gscr, cscr, *, ts, nseq):
    first = (pl.program_id(0) % nseq) == 0
    h = pa_ref[:, 0:256] * jax.nn.sigmoid(pa_ref[:, 256:512])
    hh = ph_ref[:, 0:256] * jax.nn.sigmoid(ph_ref[:, 256:512])
    hscr[0:A_HALO, :] = jnp.where(first, 0.0, hh)
    hscr[A_HALO:A_HALO + ts, :] = h
    w = dww_ref[...]
    n8 = ts // SUBLANES + 1
    blk = SUBLANES * max(d for d in range(1, 17) if n8 % d == 0)
    for b in range(SUBLANES):
        taps = [(a, A_KERNEL - 1 - (SUBLANES * a + b)) for a in range(A_HALO // SUBLANES)
                if SUBLANES * a + b < A_KERNEL]
        for g0 in range(0, ts + SUBLANES, blk):
            acc = None
            for a, j in taps:
                s0 = g0 + A_HALO - SUBLANES - SUBLANES * a
                term = w[j:j + 1, :] * hscr[s0:s0 + blk, :]
                acc = term if acc is None else acc + term
            gscr[b, g0:g0 + blk, :] = acc
    rows = 64
    for r in range(ts // rows):
        acc = jnp.broadcast_to(dwb_ref[...], (rows, GROUP_W))
        for b in range(SUBLANES):
            s0 = SUBLANES + r * rows - b
            acc = acc + gscr[b, s0:s0 + rows, :]
        cscr[r * rows:(r + 1) * rows, :] = acc
    c = cscr[...]
    mu = jnp.mean(c, axis=-1, keepdims=True)
    d = c - mu
    var = jnp.mean(d * d, axis=-1, keepdims=True)
    y = d * lax.rsqrt(var + LN_EPS) * lng_ref[...] + lnb_ref[...]
    y = _silu(y).astype(BF16)
    y = jnp.dot(y, pww_ref[...], preferred_element_type=F32) + pwb_ref[...]
    o_ref[...] = (y * _silu(pa_ref[:, 512:768])).astype(o_ref.dtype)


def _branch_a(pa, dww, dwb, lng, lnb, pww, pwb, l, seq, ts):
    t = pa.shape[0]
    nseq = seq // ts
    hb = ts // A_HALO
    return pl.pallas_call(
        functools.partial(_a_body, ts=ts, nseq=nseq),
        grid=(t // ts,),
        in_specs=[pl.BlockSpec((ts, W_PA), lambda i: (i, 0)),
                  pl.BlockSpec((A_HALO, W_PA), lambda i: (jnp.maximum(i * hb - 1, 0), 0)),
                  _layer_spec(dww, l), _layer_spec(dwb, l), _layer_spec(lng, l),
                  _layer_spec(lnb, l), _layer_spec(pww, l), _layer_spec(pwb, l)],
        out_specs=pl.BlockSpec((ts, GROUP_W), lambda i: (i, 0)),
        out_shape=jax.ShapeDtypeStruct((t, GROUP_W), BF16),
        scratch_shapes=[pltpu.VMEM((A_HALO + ts, GROUP_W), F32),
                        pltpu.VMEM((SUBLANES, ts + SUBLANES, GROUP_W), F32),
                        pltpu.VMEM((ts, GROUP_W), F32)],
        compiler_params=_params("arbitrary"),
        name="branch_a",
    )(pa, pa, dww, dwb, lng, lnb, pww, pwb)


def _group_tril(n, group):
    r = jnp.arange(n)[:, None]
    c = jnp.arange(n)[None, :]
    return ((r // group == c // group) & (c <= r)).astype(BF16)


def _b_body(pb_ref, lbl_ref, ng_ref, tril_ref, o_ref, st_ref, kscr, bscr, oscr, *, l, tb):
    @pl.when(pl.program_id(1) == 0)
    def _():
        st_ref[...] = jnp.zeros_like(st_ref)

    lg = lbl_ref[...]
    e = jnp.exp(lg - jnp.max(lg, axis=0, keepdims=True))
    p = e / jnp.sum(e, axis=0, keepdims=True)
    lb = jnp.zeros((1, GROUP_W), F32)
    for m in range(1, l + 1):
        lb = lb + p[m:m + 1, :]

    zf = pb_ref[:, 256:512]
    fg = lb + (1.0 - lb) * jax.nn.sigmoid(zf)
    logf = jnp.log(jnp.maximum(fg, GATE_FLOOR))
    kscr[...] = (1.0 - lb) * jax.nn.sigmoid(-zf)
    bscr[...] = _dot_sel(tril_ref[...], logf) * LOG2E

    hr = lax.broadcasted_iota(jnp.int32, (GROUP_W, GROUP_W), 0) // B_DK
    hc = lax.broadcasted_iota(jnp.int32, (GROUP_W, GROUP_W), 1) // B_DK
    same_head = hr == hc
    bd = jnp.where(same_head, 1.0, 0.0).astype(BF16)
    half = SUB // 2
    tio = lax.broadcasted_iota(jnp.int32, (SUB, GROUP_W), 0)
    tio_hi = lax.broadcasted_iota(jnp.int32, (half, GROUP_W), 0) + half
    zero_lo = jnp.zeros((half, GROUP_W), F32)

    def intra(qj, kj, vj, bj):
        ws = []
        for s in range(SUB):
            if s < half:
                w = jnp.where(tio >= s, qj * kj[s:s + 1, :] * jnp.exp2(bj - bj[s:s + 1, :]), 0.0)
            else:
                w = jnp.where(tio_hi >= s, qj[half:] * kj[s:s + 1, :] * jnp.exp2(bj[half:] - bj[s:s + 1, :]), 0.0)
                w = jnp.concatenate([zero_lo, w], axis=0)
            ws.append(w.astype(BF16))
        m = jnp.dot(jnp.concatenate(ws, axis=0), bd, preferred_element_type=F32)
        o_lo = jnp.zeros((half, GROUP_W), F32)
        o_hi = jnp.zeros((half, GROUP_W), F32)
        for s in range(SUB):
            vrow = vj[s:s + 1, :]
            if s < half:
                o_lo = o_lo + m[s * SUB:s * SUB + half, :] * vrow
            o_hi = o_hi + m[s * SUB + half:(s + 1) * SUB, :] * vrow
        return jnp.concatenate([o_lo, o_hi], axis=0)

    def state_step(st, qj, kj, vj, bj):
        qt = (qj * jnp.exp2(bj)).astype(BF16)
        o = lax.dot_general(qt, st.astype(BF16), NT, preferred_element_type=F32)
        blast = bj[SUB - 1:SUB, :]
        kh = (kj * jnp.exp2(blast - bj)).astype(BF16)
        u = lax.dot_general(vj.astype(BF16), kh, TN, preferred_element_type=F32)
        return o, st * jnp.exp2(blast) + jnp.where(same_head, u, 0.0)

    group = 8

    def trip(jj, carry):
        r0 = pl.multiple_of(jj * (group * SUB), group * SUB)
        rows = pl.ds(r0, group * SUB)
        qg, vg = pb_ref[rows, 0:256], pb_ref[rows, 512:768]
        kg, bg = kscr[rows, :], bscr[rows, :]
        st = st_ref[...]
        parts = [slice(n * SUB, (n + 1) * SUB) for n in range(group)]
        outs = [intra(qg[p], kg[p], vg[p], bg[p]) for p in parts]
        for n, p in enumerate(parts):
            o_inter, st = state_step(st, qg[p], kg[p], vg[p], bg[p])
            outs[n] = outs[n] + o_inter
        oscr[rows, :] = jnp.concatenate(outs, axis=0)
        st_ref[...] = st
        return carry

    lax.fori_loop(0, tb // (group * SUB), trip, 0)

    o = oscr[...]
    ms = _dot_sel(bd, o * o, sel_on_right=True) * (1.0 / B_DK)
    y = o * lax.rsqrt(ms + NORM_EPS) * ng_ref[...]
    o_ref[...] = (y * _silu(pb_ref[:, 768:1024])).astype(o_ref.dtype)


def _branch_b(pb, lbl, ng, l, batch, seq, tb):
    t = pb.shape[0]
    nseq = seq // tb
    return pl.pallas_call(
        functools.partial(_b_body, l=l, tb=tb),
        grid=(batch, nseq),
        in_specs=[pl.BlockSpec((tb, W_PB), lambda b, j: (b * nseq + j, 0)),
                  pl.BlockSpec(lbl.shape, lambda b, j: (0, 0)),
                  _layer_spec(ng, l),
                  pl.BlockSpec((tb, tb), lambda b, j: (0, 0))],
        out_specs=pl.BlockSpec((tb, GROUP_W), lambda b, j: (b * nseq + j, 0)),
        out_shape=jax.ShapeDtypeStruct((t, GROUP_W), BF16),
        scratch_shapes=[pltpu.VMEM((GROUP_W, GROUP_W), F32),
                        pltpu.VMEM((tb, GROUP_W), F32),
                        pltpu.VMEM((tb, GROUP_W), F32),
                        pltpu.VMEM((tb, GROUP_W), F32)],
        compiler_params=_params("arbitrary", "arbitrary"),
        name="branch_b",
    )(pb, lbl, ng, _group_tril(tb, SUB))


C_HALO = 8


def _c_body(pc_ref, ph_ref, pm_ref, cw_ref, cb_ref, dtb_ref, alog_ref, dsk_ref, ng_ref, tril_ref, same_ref,
            exp_ref, o_ref, h_ref, xscr, cscr, aescr, lmscr, xdscr, yscr, *, tc):
    first = pl.program_id(1) == 0

    @pl.when(first)
    def _():
        h_ref[...] = jnp.zeros_like(h_ref)

    xscr[0:C_HALO, :] = jnp.where(first, 0.0, ph_ref[:, 0:C_XBC])
    xscr[C_HALO:C_HALO + tc, :] = pc_ref[:, 0:C_XBC]
    rows = 64
    base = C_HALO - (C_CONV - 1)
    for lb in range(C_XBC // GROUP_W):
        ls = slice(lb * GROUP_W, (lb + 1) * GROUP_W)
        w = cw_ref[:, ls]
        bias = jnp.broadcast_to(cb_ref[:, ls], (rows, GROUP_W))
        for r in range(tc // rows):
            acc = bias
            for j in range(C_CONV):
                s0 = r * rows + base + j
                acc = acc + w[j:j + 1, :] * xscr[s0:s0 + rows, ls]
            cscr[r * rows:(r + 1) * rows, ls] = _silu(acc)

    dt = jnp.logaddexp(pm_ref[:, 0:LANES] + dtb_ref[...], 0.0)
    dte = _dot_sel(exp_ref[...], dt, sel_on_right=True)
    xdscr[...] = cscr[:, 0:GROUP_W] * dte
    a = dte * (-jnp.exp(alog_ref[...]))
    tb = tril_ref.shape[0]
    tpos = lax.broadcasted_iota(jnp.int32, (tb, GROUP_W), 0) % CHUNK
    spos = lax.broadcasted_iota(jnp.int32, (tb, GROUP_W), 1) % C_HEADDIM
    for r in range(tc // tb):
        rows = slice(r * tb, (r + 1) * tb)
        acs = _dot_sel(tril_ref[...], a[rows, :])
        aescr[rows, :] = acs
        acs_t = _dot_sel(same_ref[...], jnp.where(tpos == spos, acs, 0.0))
        lmscr[rows, :] = jnp.where(tpos >= spos, jnp.exp(jnp.minimum(acs - acs_t, 0.0)), 0.0)

    e = C_HEADS // C_GROUPS
    gw = e * C_HEADDIM
    first_head = lax.broadcasted_iota(jnp.int32, (CHUNK, gw), 1) < C_HEADDIM
    dsk = dsk_ref[...]

    def chunk(c, carry):
        r0 = pl.multiple_of(c * CHUNK, CHUNK)
        rs = pl.ds(r0, CHUNK)
        ae = aescr[rs, :]
        alast = ae[CHUNK - 1:CHUNK, :]
        xs = cscr[rs, 0:GROUP_W]
        xdt = xdscr[rs, :]
        wx = xdt * jnp.exp(alast - ae)
        for g in range(C_GROUPS):
            ls = slice(g * gw, (g + 1) * gw)
            bmg = cscr[rs, GROUP_W + g * C_STATE:GROUP_W + (g + 1) * C_STATE].astype(BF16)
            c0 = GROUP_W + C_GROUPS * C_STATE + g * C_STATE
            cmg = cscr[rs, c0:c0 + C_STATE].astype(BF16)
            cb2 = lax.dot_general(cmg, jnp.concatenate([bmg, bmg], axis=0), NT, preferred_element_type=F32)
            xg = xdt[:, ls]
            x2 = jnp.concatenate([jnp.where(first_head, xg, 0.0), jnp.where(first_head, 0.0, xg)], axis=0)
            ydiag = jnp.dot((cb2 * lmscr[rs, ls]).astype(BF16), x2.astype(BF16), preferred_element_type=F32)
            ht = h_ref[g * C_STATE:(g + 1) * C_STATE, :]
            yoff = jnp.dot(cmg, ht.astype(BF16), preferred_element_type=F32) * jnp.exp(ae[:, ls])
            stt = lax.dot_general(bmg, wx[:, ls].astype(BF16), TN, preferred_element_type=F32)
            h_ref[g * C_STATE:(g + 1) * C_STATE, :] = ht * jnp.exp(alast[:, ls]) + stt
            yscr[rs, ls] = ydiag + yoff + xs[:, ls] * dsk[:, ls]
        return carry

    lax.fori_loop(0, tc // CHUNK, chunk, 0, unroll=2)

    yz = yscr[...] * _silu(pc_ref[:, C_XBC:C_XBC + GROUP_W])
    gw = e * C_HEADDIM
    for g in range(C_GROUPS):
        o_ref[:, g * gw:(g + 1) * gw] = _rms(yz[:, g * gw:(g + 1) * gw],
                                             ng_ref[:, g * gw:(g + 1) * gw]).astype(o_ref.dtype)


def _branch_c(pc, pm, cw, cb, dtb, alog, dsk, ng, l, batch, seq, tc):
    t = pc.shape[0]
    nseq = seq // tc
    hb = tc // C_HALO
    tt = min(tc, 256)
    return pl.pallas_call(
        functools.partial(_c_body, tc=tc),
        grid=(batch, nseq),
        in_specs=[pl.BlockSpec((tc, W_PC), lambda b, j: (b * nseq + j, 0)),
                  pl.BlockSpec((C_HALO, W_PC), lambda b, j: (jnp.maximum((b * nseq + j) * hb - 1, 0), 0)),
                  pl.BlockSpec((tc, W_PM), lambda b, j: (b * nseq + j, 0)),
                  _layer_spec(cw, l), _layer_spec(cb, l), _layer_spec(dtb, l),
                  _layer_spec(alog, l), _layer_spec(dsk, l), _layer_spec(ng, l),
                  pl.BlockSpec((tt, tt), lambda b, j: (0, 0)),
                  pl.BlockSpec((tt, tt), lambda b, j: (0, 0)),
                  pl.BlockSpec((LANES, GROUP_W), lambda b, j: (0, 0))],
        out_specs=pl.BlockSpec((tc, GROUP_W), lambda b, j: (b * nseq + j, 0)),
        out_shape=jax.ShapeDtypeStruct((t, GROUP_W), BF16),
        scratch_shapes=[pltpu.VMEM((C_GROUPS * C_STATE, C_HEADS // C_GROUPS * C_HEADDIM), F32),
                        pltpu.VMEM((C_HALO + tc, C_XBC), F32),
                        pltpu.VMEM((tc, C_XBC), F32),
                        pltpu.VMEM((tc, GROUP_W), F32),
                        pltpu.VMEM((tc, GROUP_W), F32),
                        pltpu.VMEM((tc, GROUP_W), F32),
                        pltpu.VMEM((tc, GROUP_W), F32)],
        compiler_params=_params("arbitrary", "arbitrary"),
        name="branch_c",
    )(pc, pc, pm, cw, cb, dtb, alog, dsk, ng, _group_tril(tt, CHUNK), _group_ones(tt, CHUNK), _head_expand())


def _group_ones(n, group):
    r = jnp.arange(n)[:, None] // group
    c = jnp.arange(n)[None, :] // group
    return (r == c).astype(BF16)


def _head_expand():
    r = jnp.arange(LANES)[:, None]
    c = jnp.arange(GROUP_W)[None, :]
    return (r == c // C_HEADDIM).astype(BF16)


HEAD_W = 128
V_ROWS = 80
QK_SCALE = (D_NOPE + D_ROPE) ** -0.5
LOG2E = 1.4426950408889634


def _rope_body(posc_ref, posr_ref, fr_ref, frc_ref, cs_ref, sn_ref, cst_ref, snt_ref):
    ang = posc_ref[...] * fr_ref[...]
    cs_ref[...] = jnp.cos(ang)
    sn_ref[...] = jnp.sin(ang)
    angt = frc_ref[...] * posr_ref[...]
    cst_ref[...] = jnp.cos(angt)
    snt_ref[...] = jnp.sin(angt)


def _rope_tables(posc, posr, fr, frc, tm):
    t = posc.shape[0]
    rows = pl.BlockSpec((tm, LANES), lambda i: (i, 0))
    cols = pl.BlockSpec((D_ROPE, tm), lambda i: (0, i))
    return pl.pallas_call(
        _rope_body,
        grid=(t // tm,),
        in_specs=[pl.BlockSpec((tm, 1), lambda i: (i, 0)), pl.BlockSpec((1, tm), lambda i: (0, i)),
                  pl.BlockSpec(fr.shape, lambda i: (0, 0)), pl.BlockSpec(frc.shape, lambda i: (0, 0))],
        out_specs=[rows, rows, cols, cols],
        out_shape=[jax.ShapeDtypeStruct((t, LANES), F32)] * 2 + [jax.ShapeDtypeStruct((D_ROPE, t), F32)] * 2,
        compiler_params=_params("parallel"),
        name="rope_tables",
    )(posc, posr, fr, frc)


def _d_prep_body(pd_ref, pm_ref, cs_ref, sn_ref, cst_ref, snt_ref, qag_ref, wqt_ref, wqrt_ref, kvag_ref,
                 wk_ref, wvt_ref, qt_ref, k_ref, vt_ref):
    sc = QK_SCALE * LOG2E
    hq = _rms(pd_ref[:, 256:512], qag_ref[...]).astype(BF16)
    q0 = lax.dot_general(wqt_ref[...], hq, NT, preferred_element_type=F32)
    q1 = lax.dot_general(wqrt_ref[...], hq, NT, preferred_element_type=F32)
    cst = cst_ref[...]
    snt = snt_ref[...]
    for h in range(D_HEADS):
        r0 = h * HEAD_W
        qt_ref[h, 0:D_NOPE, :] = (q0[r0:r0 + D_NOPE] * sc).astype(BF16)
        rope = q0[r0 + D_NOPE:r0 + D_NOPE + D_ROPE] * cst + q1[h * D_ROPE:(h + 1) * D_ROPE] * snt
        qt_ref[h, D_NOPE:D_NOPE + D_ROPE, :] = (rope * sc).astype(BF16)
        qt_ref[h, D_NOPE + D_ROPE:HEAD_W, :] = jnp.zeros((HEAD_W - D_NOPE - D_ROPE, q0.shape[1]), BF16)

    hkv = _rms(pd_ref[:, 512:640], kvag_ref[...]).astype(BF16)
    k0 = jnp.dot(hkv, wk_ref[...], preferred_element_type=F32)
    cs = cs_ref[...]
    sn = sn_ref[...]
    lane = lax.broadcasted_iota(jnp.int32, cs.shape, 1)
    rope_lane = (lane >= D_NOPE) & (lane < D_NOPE + D_ROPE)
    kr = jnp.where(rope_lane, pm_ref[:, 0:LANES] * cs, 0.0) + pm_ref[:, LANES:2 * LANES] * sn
    k_ref[...] = (k0 + jnp.concatenate([kr] * D_HEADS, axis=1)).astype(BF16)

    vt = lax.dot_general(wvt_ref[...], hkv, NT, preferred_element_type=F32)
    pad = V_ROWS - D_VDIM
    ones_row = jnp.where(lax.broadcasted_iota(jnp.int32, (pad, vt.shape[1]), 0) == 0, 1.0, 0.0).astype(BF16)
    for h in range(D_HEADS):
        vt_ref[h, 0:D_VDIM, :] = vt[h * D_VDIM:(h + 1) * D_VDIM].astype(BF16)
        vt_ref[h, D_VDIM:V_ROWS, :] = ones_row


def _d_prep(pd, pm, rope, qag, wqt, wqrt, kvag, wk, wvt, l, batch, seq, tm):
    t = pd.shape[0]
    nseq = seq // tm
    cs, sn, cst, snt = rope
    rows = pl.BlockSpec((tm, LANES), lambda b, i: (b * nseq + i, 0))
    cols = pl.BlockSpec((D_ROPE, tm), lambda b, i: (0, b * nseq + i))
    return pl.pallas_call(
        _d_prep_body,
        grid=(batch, nseq),
        in_specs=[pl.BlockSpec((tm, W_PD), lambda b, i: (b * nseq + i, 0)),
                  pl.BlockSpec((tm, W_PM), lambda b, i: (b * nseq + i, 0)),
                  rows, rows, cols, cols,
                  _layer_spec(qag, l), _layer_spec(wqt, l), _layer_spec(wqrt, l),
                  _layer_spec(kvag, l), _layer_spec(wk, l), _layer_spec(wvt, l)],
        out_specs=[pl.BlockSpec((None, D_HEADS, HEAD_W, tm), lambda b, i: (b, 0, 0, i)),
                   pl.BlockSpec((tm, D_HEADS * HEAD_W), lambda b, i: (b * nseq + i, 0)),
                   pl.BlockSpec((None, D_HEADS, None, V_ROWS, tm), lambda b, i: (b, 0, i, 0, 0))],
        out_shape=[jax.ShapeDtypeStruct((batch, D_HEADS, HEAD_W, seq), BF16),
                   jax.ShapeDtypeStruct((t, D_HEADS * HEAD_W), BF16),
                   jax.ShapeDtypeStruct((batch, D_HEADS, nseq, V_ROWS, tm), BF16)],
        compiler_params=_params("parallel", "parallel"),
        name="mla_prep",
    )(pd, pm, cs, sn, cst, snt, qag, wqt, wqrt, kvag, wk, wvt)


ATT_HEADS_PER_STEP = 4


def _attn_body(qt_ref, k_ref, vt_ref, o_ref, acc, mrow, sa, sb, *, tq):
    i = pl.program_id(2)
    key_chunk = lax.broadcasted_iota(jnp.int32, (tq, tq), 0) // CHUNK
    qry_chunk = lax.broadcasted_iota(jnp.int32, (tq, tq), 1) // CHUNK
    visible = key_chunk <= qry_chunk
    heads = range(ATT_HEADS_PER_STEP)

    def scores(hh, kt):
        k0 = pl.multiple_of(kt * tq, tq)
        return jnp.dot(k_ref[pl.ds(k0, tq), hh * HEAD_W:(hh + 1) * HEAD_W], qt_ref[hh],
                       preferred_element_type=F32)

    def key_tile(u):
        return jnp.where(u == 0, i, u - 1)

    def fold(hh, st, u):
        m_old = mrow[hh]
        m_new = jnp.maximum(m_old, jnp.max(st, axis=0, keepdims=True))
        pt = jnp.exp2(st - m_new).astype(BF16)
        acc[hh] = (jnp.exp2(m_old - m_new) * acc[hh]
                   + jnp.dot(vt_ref[hh, key_tile(u)], pt, preferred_element_type=F32))
        mrow[hh] = m_new

    for hh in heads:
        acc[hh] = jnp.zeros(acc.shape[1:], F32)
        mrow[hh] = jnp.full(mrow.shape[1:], MASK_VALUE, F32)
        sa[hh] = jnp.where(visible, scores(hh, i), MASK_VALUE)

    def two_tiles(u):
        for hh in heads:
            sb[hh] = scores(hh, u)
            fold(hh, sa[hh], u)
        for hh in heads:
            sa[hh] = scores(hh, u + 1)
            fold(hh, sb[hh], u + 1)

    def quad_trip(p, carry):
        two_tiles(4 * p)
        two_tiles(4 * p + 2)
        return carry

    quads = lax.shift_right_logical(i, 2)
    lax.fori_loop(0, quads, quad_trip, 0)
    u1 = 4 * quads

    def pair_trip(p, carry):
        two_tiles(u1 + 2 * p)
        return carry

    pairs = lax.shift_right_logical(i - u1, 1)
    lax.fori_loop(0, pairs, pair_trip, 0)
    u0 = u1 + 2 * pairs
    left = i + 1 - u0

    @pl.when(left == 2)
    def _():
        for hh in heads:
            sb[hh] = scores(hh, u0)
            fold(hh, sa[hh], u0)
        for hh in heads:
            fold(hh, sb[hh], u0 + 1)

    @pl.when(left == 1)
    def _():
        for hh in heads:
            fold(hh, sa[hh], u0)

    outs = []
    for hh in heads:
        a = acc[hh]
        outs.append(a[0:D_VDIM, :] / a[D_VDIM:D_VDIM + 1, :])
    o_ref[...] = jnp.concatenate(outs, axis=0).T.astype(o_ref.dtype)


def _attention(qt, k, vt, batch, seq, tq):
    t = k.shape[0]
    nq = seq // tq
    g = ATT_HEADS_PER_STEP
    return pl.pallas_call(
        functools.partial(_attn_body, tq=tq),
        grid=(batch, D_HEADS // g, nq),
        in_specs=[pl.BlockSpec((None, g, HEAD_W, tq), lambda b, h, i: (b, h, 0, i)),
                  pl.BlockSpec((seq, g * HEAD_W), lambda b, h, i: (b, h), pipeline_mode=pl.Buffered(1)),
                  pl.BlockSpec((None, g, nq, V_ROWS, tq), lambda b, h, i: (b, h, 0, 0, 0),
                               pipeline_mode=pl.Buffered(1))],
        out_specs=pl.BlockSpec((tq, g * D_VDIM), lambda b, h, i: (b * nq + i, h)),
        out_shape=jax.ShapeDtypeStruct((t, D_HEADS * D_VDIM), BF16),
        scratch_shapes=[pltpu.VMEM((g, V_ROWS, tq), F32), pltpu.VMEM((g, 1, tq), F32),
                        pltpu.VMEM((g, tq, tq), F32), pltpu.VMEM((g, tq, tq), F32)],
        compiler_params=_params("arbitrary", "arbitrary", "arbitrary"),
        name="mla_attention",
    )(qt, k, vt)


def _out_body(ya_ref, yb_ref, yc_ref, od_ref, gd_ref, x_ref, w_ref, pg_ref, o_ref):
    yd = (od_ref[...].astype(F32) * _silu(gd_ref[...])).astype(BF16)
    y = None
    for n, part in enumerate((ya_ref[...], yb_ref[...], yc_ref[...], yd)):
        t = jnp.dot(part, w_ref[n * GROUP_W:(n + 1) * GROUP_W, :], preferred_element_type=F32)
        y = t if y is None else y + t
    o_ref[...] = x_ref[...] + _rms(y, pg_ref[...])


def _outproj(ya, yb, yc, od, pd, x, w2, post_g, l, tm):
    t = x.shape[0]
    part = pl.BlockSpec((tm, GROUP_W), lambda i: (i, 0))
    return pl.pallas_call(
        _out_body,
        grid=(t // tm,),
        in_specs=[part, part, part, part, part,
                  pl.BlockSpec((tm, D_MODEL), lambda i: (i, 0)),
                  _layer_spec(w2, l), _layer_spec(post_g, l)],
        out_specs=pl.BlockSpec((tm, D_MODEL), lambda i: (i, 0)),
        out_shape=jax.ShapeDtypeStruct((t, D_MODEL), F32),
        compiler_params=_params("parallel"),
        name="outproj",
    )(ya, yb, yc, od, pd, x, w2, post_g)


def _out_in_body(ya_ref, yb_ref, yc_ref, od_ref, gd_ref, x_ref, w2_ref, pg_ref, g1_ref, w1_ref, o_ref, *outs):
    _out_body(ya_ref, yb_ref, yc_ref, od_ref, gd_ref, x_ref, w2_ref, pg_ref, o_ref)
    _inproj_body(o_ref, g1_ref, w1_ref, *outs)


def _outproj_inproj(ya, yb, yc, od, pd, x, w2, post_g, pre_g, w1, l, tm):
    t = x.shape[0]
    part = pl.BlockSpec((tm, GROUP_W), lambda i: (i, 0))
    widths = (W_PA, W_PB, W_PC, W_PD, W_PM)
    res = pl.pallas_call(
        _out_in_body,
        grid=(t // tm,),
        in_specs=[part, part, part, part, part,
                  pl.BlockSpec((tm, D_MODEL), lambda i: (i, 0)),
                  _layer_spec(w2, l), _layer_spec(post_g, l),
                  _layer_spec(pre_g, l + 1), _layer_spec(w1, l + 1)],
        out_specs=[pl.BlockSpec((tm, D_MODEL), lambda i: (i, 0))]
                  + [pl.BlockSpec((tm, n), lambda i: (i, 0)) for n in widths],
        out_shape=[jax.ShapeDtypeStruct((t, D_MODEL), F32)]
                  + [jax.ShapeDtypeStruct((t, n), F32) for n in widths],
        compiler_params=_params("parallel"),
        name="outproj_inproj",
    )(ya, yb, yc, od, pd, x, w2, post_g, pre_g, w1)
    return res[0], res[1:]


def _rot_cols(w):
    half = D_ROPE // 2
    return jnp.concatenate([-w[..., half:], w[..., :half]], axis=-1)


def _prep_w_in(w_in):
    d = w_in.shape[0]
    z = lambda n: jnp.zeros((d, D_MODEL, n), w_in.dtype)
    c = lambda a, n: w_in[:, :, a:a + n]
    kr = c(3204, D_ROPE)
    cols = [c(0, 768),
            c(768, 1024),
            c(1792, 768), c(2564, 256),
            c(3236, 256), c(2820, 256), c(3076, 128),
            c(2560, 4), z(60), kr, z(32),
            z(64), _rot_cols(kr), z(32)]
    return jnp.concatenate(cols, axis=-1).astype(BF16)


def _prep_q(qb_w):
    d = qb_w.shape[0]
    z = lambda n: jnp.zeros((d, D_Q_RANK, n), qb_w.dtype)
    plain, rot = [], []
    for h in range(D_HEADS):
        o = h * (D_NOPE + D_ROPE)
        nope, rope = qb_w[:, :, o:o + D_NOPE], qb_w[:, :, o + D_NOPE:o + D_NOPE + D_ROPE]
        plain += [nope, rope, z(HEAD_W - D_NOPE - D_ROPE)]
        rot += [_rot_cols(rope)]
    t = lambda cols: jnp.swapaxes(jnp.concatenate(cols, axis=-1), 1, 2).astype(BF16)
    return t(plain), t(rot)


def _prep_kv(kvb_w):
    d = kvb_w.shape[0]
    z = lambda n: jnp.zeros((d, D_KV_RANK, n), kvb_w.dtype)
    ks, vs = [], []
    for h in range(D_HEADS):
        o = h * (D_NOPE + D_VDIM)
        ks += [kvb_w[:, :, o:o + D_NOPE], z(HEAD_W - D_NOPE)]
        vs += [kvb_w[:, :, o + D_NOPE:o + D_NOPE + D_VDIM]]
    return (jnp.concatenate(ks, axis=-1).astype(BF16),
            jnp.swapaxes(jnp.concatenate(vs, axis=-1), 1, 2).astype(BF16))


def _row(p, width=None):
    if width is not None and width > p.shape[-1]:
        p = jnp.pad(p, ((0, 0), (0, width - p.shape[-1])))
    return p[:, None, :]


def _pick_tile(n, want):
    t = min(want, n)
    while n % t:
        t //= 2
    return t


@jax.jit
def _forward(x, positions, pre_norm_g, post_norm_g, w_in, w_out, a_dw_w, a_dw_b, a_ln_g, a_ln_b,
             a_pw_w, a_pw_b, b_lb_logits, b_norm_g, c_conv_w, c_conv_b, c_dt_bias, c_a_log, c_d,
             c_norm_g, d_qa_g, d_qb_w, d_kva_g, d_kvb_w):
    batch, seq, _ = x.shape
    depth = w_in.shape[0]
    t = batch * seq
    tm = _pick_tile(seq, 512)
    tb = _pick_tile(seq, 256)

    w1 = _prep_w_in(w_in)
    w2 = w_out.astype(BF16)
    wqt, wqrt = _prep_q(d_qb_w)
    wk, wvt = _prep_kv(d_kvb_w)
    dww = jnp.pad(a_dw_w, ((0, 0), (0, 32 - A_KERNEL), (0, 0)))
    pww = a_pw_w.astype(BF16)
    dsk = _row(jnp.repeat(c_d, C_HEADDIM, axis=-1))
    alog_e = _row(jnp.repeat(c_a_log, C_HEADDIM, axis=-1))
    inv_freq = ROPE_THETA ** (-jnp.arange(0, D_ROPE, 2, dtype=F32) / D_ROPE)
    fr = jnp.concatenate([jnp.zeros((D_NOPE,), F32), inv_freq, inv_freq,
                          jnp.zeros((HEAD_W - D_NOPE - D_ROPE,), F32)])[None, :]
    frc = jnp.concatenate([inv_freq, inv_freq])[:, None]
    posc = positions.astype(F32).reshape(t, 1)
    posr = positions.astype(F32).reshape(1, t)

    rope = _rope_tables(posc, posr, fr, frc, tm)

    xf = x.reshape(t, D_MODEL)
    pa, pb, pc, pd, pm = _inproj(xf, _row(pre_norm_g), w1, 0, tm)
    for l in range(depth):
        ya = _branch_a(pa, dww, _row(a_dw_b), _row(a_ln_g), _row(a_ln_b), pww, _row(a_pw_b), l, seq, tm)
        yb = _branch_b(pb, b_lb_logits, _row(b_norm_g), l, batch, seq, tb)
        yc = _branch_c(pc, pm, c_conv_w, _row(c_conv_b), _row(c_dt_bias, LANES), alog_e,
                       dsk, _row(c_norm_g), l, batch, seq, tm)
        qt, k, vt = _d_prep(pd, pm, rope, _row(d_qa_g), wqt, wqrt, _row(d_kva_g), wk, wvt, l, batch, seq, tm)
        od = _attention(qt, k, vt, batch, seq, tm)
        if l + 1 < depth:
            xf, (pa, pb, pc, pd, pm) = _outproj_inproj(ya, yb, yc, od, pd, xf, w2, _row(post_norm_g),
                                                       _row(pre_norm_g), w1, l, tm)
        else:
            xf = _outproj(ya, yb, yc, od, pd, xf, w2, _row(post_norm_g), l, tm)
    return xf.reshape(batch, seq, D_MODEL)


def kernel(x, positions, pre_norm_g, post_norm_g, w_in, w_out, a_dw_w, a_dw_b, a_ln_g, a_ln_b, a_pw_w, a_pw_b, b_lb_logits, b_norm_g, c_conv_w, c_conv_b, c_dt_bias, c_a_log, c_d, c_norm_g, d_qa_g, d_qb_w, d_kva_g, d_kvb_w):
    return _forward(x, positions, pre_norm_g, post_norm_g, w_in, w_out, a_dw_w, a_dw_b, a_ln_g, a_ln_b,
                    a_pw_w, a_pw_b, b_lb_logits, b_norm_g, c_conv_w, c_conv_b, c_dt_bias, c_a_log, c_d,
                    c_norm_g, d_qa_g, d_qb_w, d_kva_g, d_kvb_w)
```

```python
import functools
import math

import jax
import jax.numpy as jnp
from jax import lax
from jax.experimental import pallas as pl
from jax.experimental.pallas import tpu as pltpu

F32 = jnp.float32
BF16 = jnp.bfloat16

D_MODEL = 1024
CHUNK = 64
GROUP_W = 256
NORM_EPS = 1e-6
LN_EPS = 1e-5
MASK_VALUE = -1e30
GATE_FLOOR = 1e-20
A_KERNEL = 31
B_HEADS = 4
B_DK = 64
C_HEADS = 4
C_HEADDIM = 64
C_GROUPS = 2
C_STATE = 128
C_CONV = 4
C_XBC = 768
D_HEADS = 4
D_Q_RANK = 256
D_KV_RANK = 128
D_NOPE = 64
D_ROPE = 32
D_VDIM = 64
ROPE_THETA = 10000.0

LANES = 128
SUB = 16
VMEM_LIMIT = 48 * 1024 * 1024

W_PA, W_PB, W_PC, W_PD, W_PM = 768, 1024, 1024, 640, 256
N_PROJ = W_PA + W_PB + W_PC + W_PD + W_PM

NT = (((1,), (1,)), ((), ()))
TN = (((0,), (0,)), ((), ()))


def _params(*sem):
    return pltpu.CompilerParams(dimension_semantics=sem, vmem_limit_bytes=VMEM_LIMIT)


def _layer_spec(arr, l):
    shape = arr.shape[1:]
    nd = len(shape)
    return pl.BlockSpec((None,) + shape, lambda *_: (l,) + (0,) * nd)


def _split3(x):
    h1 = x.astype(BF16)
    r1 = x - h1.astype(F32)
    h2 = r1.astype(BF16)
    h3 = (r1 - h2.astype(F32)).astype(BF16)
    return h1, h2, h3


def _dot_sel(sel, x, dims=None, sel_on_right=False):
    if dims is None:
        dims = (((1,), (0,)), ((), ()))
    out = None
    for piece in _split3(x):
        a, b = (piece, sel) if sel_on_right else (sel, piece)
        t = lax.dot_general(a, b, dims, preferred_element_type=F32)
        out = t if out is None else out + t
    return out


def _silu(x):
    return x * jax.nn.sigmoid(x)


def _rms(x, g):
    ms = jnp.mean(x * x, axis=-1, keepdims=True)
    return x * lax.rsqrt(ms + NORM_EPS) * g


def _inproj_body(x_ref, g_ref, w_ref, *outs):
    h = _rms(x_ref[...], g_ref[...]).astype(BF16)
    off = 0
    for o in outs:
        n = o.shape[-1]
        o[...] = jnp.dot(h, w_ref[:, off:off + n], preferred_element_type=F32).astype(o.dtype)
        off += n


def _inproj(x, pre_g, w1, l, tm):
    t = x.shape[0]
    widths = (W_PA, W_PB, W_PC, W_PD, W_PM)
    return pl.pallas_call(
        _inproj_body,
        grid=(t // tm,),
        in_specs=[pl.BlockSpec((tm, D_MODEL), lambda i: (i, 0)),
                  _layer_spec(pre_g, l), _layer_spec(w1, l)],
        out_specs=[pl.BlockSpec((tm, n), lambda i: (i, 0)) for n in widths],
        out_shape=[jax.ShapeDtypeStruct((t, n), F32) for n in widths],
        compiler_params=_params("parallel"),
        name="inproj",
    )(x, pre_g, w1)


A_HALO = 32


SUBLANES = 8


def _a_body(pa_ref, ph_ref, dww_ref, dwb_ref, lng_ref, lnb_ref, pww_ref, pwb_ref, o_ref,
            hscr, gscr, cscr, *, ts, nseq):
    first = (pl.program_id(0) % nseq) == 0
    h = pa_ref[:, 0:256] * jax.nn.sigmoid(pa_ref[:, 256:512])
    hh = ph_ref[:, 0:256] * jax.nn.sigmoid(ph_ref[:, 256:512])
    hscr[0:A_HALO, :] = jnp.where(first, 0.0, hh)
    hscr[A_HALO:A_HALO + ts, :] = h
    w = dww_ref[...]
    n8 = ts // SUBLANES + 1
    blk = SUBLANES * max(d for d in range(1, 17) if n8 % d == 0)
    for b in range(SUBLANES):
        taps = [(a, A_KERNEL - 1 - (SUBLANES * a + b)) for a in range(A_HALO // SUBLANES)
                if SUBLANES * a + b < A_KERNEL]
        for g0 in range(0, ts + SUBLANES, blk):
            acc = None
            for a, j in taps:
                s0 = g0 + A_HALO - SUBLANES - SUBLANES * a
                term = w[j:j + 1, :] * hscr[s0:s0 + blk, :]
                acc = term if acc is None else acc + term
            gscr[b, g0:g0 + blk, :] = acc
    rows = 64
    for r in range(ts // rows):
        acc = jnp.broadcast_to(dwb_ref[...], (rows, GROUP_W))
        for b in range(SUBLANES):
            s0 = SUBLANES + r * rows - b
            acc = acc + gscr[b, s0:s0 + rows, :]
        cscr[r * rows:(r + 1) * rows, :] = acc
    c = cscr[...]
    mu = jnp.mean(c, axis=-1, keepdims=True)
    d = c - mu
    var = jnp.mean(d * d, axis=-1, keepdims=True)
    y = d * lax.rsqrt(var + LN_EPS) * lng_ref[...] + lnb_ref[...]
    y = _silu(y).astype(BF16)
    y = jnp.dot(y, pww_ref[...], preferred_element_type=F32) + pwb_ref[...]
    o_ref[...] = (y * _silu(pa_ref[:, 512:768])).astype(o_ref.dtype)


def _branch_a(pa, dww, dwb, lng, lnb, pww, pwb, l, seq, ts):
    t = pa.shape[0]
    nseq = seq // ts
    hb = ts // A_HALO
    return pl.pallas_call(
        functools.partial(_a_body, ts=ts, nseq=nseq),
        grid=(t // ts,),
        in_specs=[pl.BlockSpec((ts, W_PA), lambda i: (i, 0)),
                  pl.BlockSpec((A_HALO, W_PA), lambda i: (jnp.maximum(i * hb - 1, 0), 0)),
                  _layer_spec(dww, l), _layer_spec(dwb, l), _layer_spec(lng, l),
                  _layer_spec(lnb, l), _layer_spec(pww, l), _layer_spec(pwb, l)],
        out_specs=pl.BlockSpec((ts, GROUP_W), lambda i: (i, 0)),
        out_shape=jax.ShapeDtypeStruct((t, GROUP_W), BF16),
        scratch_shapes=[pltpu.VMEM((A_HALO + ts, GROUP_W), F32),
                        pltpu.VMEM((SUBLANES, ts + SUBLANES, GROUP_W), F32),
                        pltpu.VMEM((ts, GROUP_W), F32)],
        compiler_params=_params("arbitrary"),
        name="branch_a",
    )(pa, pa, dww, dwb, lng, lnb, pww, pwb)


def _group_tril(n, group):
    r = jnp.arange(n)[:, None]
    c = jnp.arange(n)[None, :]
    return ((r // group == c // group) & (c <= r)).astype(BF16)


def _b_body(pb_ref, lbl_ref, ng_ref, tril_ref, o_ref, st_ref, kscr, bscr, oscr, *, l, tb):
    @pl.when(pl.program_id(1) == 0)
    def _():
        st_ref[...] = jnp.zeros_like(st_ref)

    lg = lbl_ref[...]
    e = jnp.exp(lg - jnp.max(lg, axis=0, keepdims=True))
    p = e / jnp.sum(e, axis=0, keepdims=True)
    lb = jnp.zeros((1, GROUP_W), F32)
    for m in range(1, l + 1):
        lb = lb + p[m:m + 1, :]

    zf = pb_ref[:, 256:512]
    fg = lb + (1.0 - lb) * jax.nn.sigmoid(zf)
    logf = jnp.log(jnp.maximum(fg, GATE_FLOOR))
    kscr[...] = (1.0 - lb) * jax.nn.sigmoid(-zf)
    bscr[...] = _dot_sel(tril_ref[...], logf) * LOG2E

    hr = lax.broadcasted_iota(jnp.int32, (GROUP_W, GROUP_W), 0) // B_DK
    hc = lax.broadcasted_iota(jnp.int32, (GROUP_W, GROUP_W), 1) // B_DK
    same_head = hr == hc
    bd = jnp.where(same_head, 1.0, 0.0).astype(BF16)
    half = SUB // 2
    tio = lax.broadcasted_iota(jnp.int32, (SUB, GROUP_W), 0)
    tio_hi = lax.broadcasted_iota(jnp.int32, (half, GROUP_W), 0) + half
    zero_lo = jnp.zeros((half, GROUP_W), F32)

    def intra(qj, kj, vj, bj):
        ws = []
        for s in range(SUB):
            if s < half:
                w = jnp.where(tio >= s, qj * kj[s:s + 1, :] * jnp.exp2(bj - bj[s:s + 1, :]), 0.0)
            else:
                w = jnp.where(tio_hi >= s, qj[half:] * kj[s:s + 1, :] * jnp.exp2(bj[half:] - bj[s:s + 1, :]), 0.0)
                w = jnp.concatenate([zero_lo, w], axis=0)
            ws.append(w.astype(BF16))
        m = jnp.dot(jnp.concatenate(ws, axis=0), bd, preferred_element_type=F32)
        o_lo = jnp.zeros((half, GROUP_W), F32)
        o_hi = jnp.zeros((half, GROUP_W), F32)
        for s in range(SUB):
            vrow = vj[s:s + 1, :]
            if s < half:
                o_lo = o_lo + m[s * SUB:s * SUB + half, :] * vrow
            o_hi = o_hi + m[s * SUB + half:(s + 1) * SUB, :] * vrow
        return jnp.concatenate([o_lo, o_hi], axis=0)

    def state_step(st, qj, kj, vj, bj):
        qt = (qj * jnp.exp2(bj)).astype(BF16)
        o = lax.dot_general(qt, st.astype(BF16), NT, preferred_element_type=F32)
        blast = bj[SUB - 1:SUB, :]
        kh = (kj * jnp.exp2(blast - bj)).astype(BF16)
        u = lax.dot_general(vj.astype(BF16), kh, TN, preferred_element_type=F32)
        return o, st * jnp.exp2(blast) + jnp.where(same_head, u, 0.0)

    group = 8

    def trip(jj, carry):
        r0 = pl.multiple_of(jj * (group * SUB), group * SUB)
        rows = pl.ds(r0, group * SUB)
        qg, vg = pb_ref[rows, 0:256], pb_ref[rows, 512:768]
        kg, bg = kscr[rows, :], bscr[rows, :]
        st = st_ref[...]
        parts = [slice(n * SUB, (n + 1) * SUB) for n in range(group)]
        outs = [intra(qg[p], kg[p], vg[p], bg[p]) for p in parts]
        for n, p in enumerate(parts):
            o_inter, st = state_step(st, qg[p], kg[p], vg[p], bg[p])
            outs[n] = outs[n] + o_inter
        oscr[rows, :] = jnp.concatenate(outs, axis=0)
        st_ref[...] = st
        return carry

    lax.fori_loop(0, tb // (group * SUB), trip, 0)

    o = oscr[...]
    ms = _dot_sel(bd, o * o, sel_on_right=True) * (1.0 / B_DK)
    y = o * lax.rsqrt(ms + NORM_EPS) * ng_ref[...]
    o_ref[...] = (y * _silu(pb_ref[:, 768:1024])).astype(o_ref.dtype)


def _branch_b(pb, lbl, ng, l, batch, seq, tb):
    t = pb.shape[0]
    nseq = seq // tb
    return pl.pallas_call(
        functools.partial(_b_body, l=l, tb=tb),
        grid=(batch, nseq),
        in_specs=[pl.BlockSpec((tb, W_PB), lambda b, j: (b * nseq + j, 0)),
                  pl.BlockSpec(lbl.shape, lambda b, j: (0, 0)),
                  _layer_spec(ng, l),
                  pl.BlockSpec((tb, tb), lambda b, j: (0, 0))],
        out_specs=pl.BlockSpec((tb, GROUP_W), lambda b, j: (b * nseq + j, 0)),
        out_shape=jax.ShapeDtypeStruct((t, GROUP_W), BF16),
        scratch_shapes=[pltpu.VMEM((GROUP_W, GROUP_W), F32),
                        pltpu.VMEM((tb, GROUP_W), F32),
                        pltpu.VMEM((tb, GROUP_W), F32),
                        pltpu.VMEM((tb, GROUP_W), F32)],
        compiler_params=_params("arbitrary", "arbitrary"),
        name="branch_b",
    )(pb, lbl, ng, _group_tril(tb, SUB))


C_HALO = 8


def _c_body(pc_ref, ph_ref, pm_ref, cw_ref, cb_ref, dtb_ref, alog_ref, dsk_ref, ng_ref, tril_ref, same_ref,
            exp_ref, o_ref, h_ref, xscr, cscr, aescr, lmscr, xdscr, yscr, *, tc):
    first = pl.program_id(1) == 0

    @pl.when(first)
    def _():
        h_ref[...] = jnp.zeros_like(h_ref)

    xscr[0:C_HALO, :] = jnp.where(first, 0.0, ph_ref[:, 0:C_XBC])
    xscr[C_HALO:C_HALO + tc, :] = pc_ref[:, 0:C_XBC]
    rows = 64
    base = C_HALO - (C_CONV - 1)
    for lb in range(C_XBC // GROUP_W):
        ls = slice(lb * GROUP_W, (lb + 1) * GROUP_W)
        w = cw_ref[:, ls]
        bias = jnp.broadcast_to(cb_ref[:, ls], (rows, GROUP_W))
        for r in range(tc // rows):
            acc = bias
            for j in range(C_CONV):
                s0 = r * rows + base + j
                acc = acc + w[j:j + 1, :] * xscr[s0:s0 + rows, ls]
            cscr[r * rows:(r + 1) * rows, ls] = _silu(acc)

    dt = jnp.logaddexp(pm_ref[:, 0:LANES] + dtb_ref[...], 0.0)
    dte = _dot_sel(exp_ref[...], dt, sel_on_right=True)
    xdscr[...] = cscr[:, 0:GROUP_W] * dte
    a = dte * (-jnp.exp(alog_ref[...]))
    tb = tril_ref.shape[0]
    tpos = lax.broadcasted_iota(jnp.int32, (tb, GROUP_W), 0) % CHUNK
    spos = lax.broadcasted_iota(jnp.int32, (tb, GROUP_W), 1) % C_HEADDIM
    for r in range(tc // tb):
        rows = slice(r * tb, (r + 1) * tb)
        acs = _dot_sel(tril_ref[...], a[rows, :])
        aescr[rows, :] = acs
        acs_t = _dot_sel(same_ref[...], jnp.where(tpos == spos, acs, 0.0))
        lmscr[rows, :] = jnp.where(tpos >= spos, jnp.exp(jnp.minimum(acs - acs_t, 0.0)), 0.0)

    e = C_HEADS // C_GROUPS
    gw = e * C_HEADDIM
    first_head = lax.broadcasted_iota(jnp.int32, (CHUNK, gw), 1) < C_HEADDIM
    dsk = dsk_ref[...]

    def chunk(c, carry):
        r0 = pl.multiple_of(c * CHUNK, CHUNK)
        rs = pl.ds(r0, CHUNK)
        ae = aescr[rs, :]
        alast = ae[CHUNK - 1:CHUNK, :]
        xs = cscr[rs, 0:GROUP_W]
        xdt = xdscr[rs, :]
        wx = xdt * jnp.exp(alast - ae)
        for g in range(C_GROUPS):
            ls = slice(g * gw, (g + 1) * gw)
            bmg = cscr[rs, GROUP_W + g * C_STATE:GROUP_W + (g + 1) * C_STATE].astype(BF16)
            c0 = GROUP_W + C_GROUPS * C_STATE + g * C_STATE
            cmg = cscr[rs, c0:c0 + C_STATE].astype(BF16)
            cb2 = lax.dot_general(cmg, jnp.concatenate([bmg, bmg], axis=0), NT, preferred_element_type=F32)
            xg = xdt[:, ls]
            x2 = jnp.concatenate([jnp.where(first_head, xg, 0.0), jnp.where(first_head, 0.0, xg)], axis=0)
            ydiag = jnp.dot((cb2 * lmscr[rs, ls]).astype(BF16), x2.astype(BF16), preferred_element_type=F32)
            ht = h_ref[g * C_STATE:(g + 1) * C_STATE, :]
            yoff = jnp.dot(cmg, ht.astype(BF16), preferred_element_type=F32) * jnp.exp(ae[:, ls])
            stt = lax.dot_general(bmg, wx[:, ls].astype(BF16), TN, preferred_element_type=F32)
            h_ref[g * C_STATE:(g + 1) * C_STATE, :] = ht * jnp.exp(alast[:, ls]) + stt
            yscr[rs, ls] = ydiag + yoff + xs[:, ls] * dsk[:, ls]
        return carry

    lax.fori_loop(0, tc // CHUNK, chunk, 0, unroll=2)

    yz = yscr[...] * _silu(pc_ref[:, C_XBC:C_XBC + GROUP_W])
    gw = e * C_HEADDIM
    for g in range(C_GROUPS):
        o_ref[:, g * gw:(g + 1) * gw] = _rms(yz[:, g * gw:(g + 1) * gw],
                                             ng_ref[:, g * gw:(g + 1) * gw]).astype(o_ref.dtype)


def _branch_c(pc, pm, cw, cb, dtb, alog, dsk, ng, l, batch, seq, tc):
    t = pc.shape[0]
    nseq = seq // tc
    hb = tc // C_HALO
    tt = min(tc, 256)
    return pl.pallas_call(
        functools.partial(_c_body, tc=tc),
        grid=(batch, nseq),
        in_specs=[pl.BlockSpec((tc, W_PC), lambda b, j: (b * nseq + j, 0)),
                  pl.BlockSpec((C_HALO, W_PC), lambda b, j: (jnp.maximum((b * nseq + j) * hb - 1, 0), 0)),
                  pl.BlockSpec((tc, W_PM), lambda b, j: (b * nseq + j, 0)),
                  _layer_spec(cw, l), _layer_spec(cb, l), _layer_spec(dtb, l),
                  _layer_spec(alog, l), _layer_spec(dsk, l), _layer_spec(ng, l),
                  pl.BlockSpec((tt, tt), lambda b, j: (0, 0)),
                  pl.BlockSpec((tt, tt), lambda b, j: (0, 0)),
                  pl.BlockSpec((LANES, GROUP_W), lambda b, j: (0, 0))],
        out_specs=pl.BlockSpec((tc, GROUP_W), lambda b, j: (b * nseq + j, 0)),
        out_shape=jax.ShapeDtypeStruct((t, GROUP_W), BF16),
        scratch_shapes=[pltpu.VMEM((C_GROUPS * C_STATE, C_HEADS // C_GROUPS * C_HEADDIM), F32),
                        pltpu.VMEM((C_HALO + tc, C_XBC), F32),
                        pltpu.VMEM((tc, C_XBC), F32),
                        pltpu.VMEM((tc, GROUP_W), F32),
                        pltpu.VMEM((tc, GROUP_W), F32),
                        pltpu.VMEM((tc, GROUP_W), F32),
                        pltpu.VMEM((tc, GROUP_W), F32)],
        compiler_params=_params("arbitrary", "arbitrary"),
        name="branch_c",
    )(pc, pc, pm, cw, cb, dtb, alog, dsk, ng, _group_tril(tt, CHUNK), _group_ones(tt, CHUNK), _head_expand())


def _group_ones(n, group):
    r = jnp.arange(n)[:, None] // group
    c = jnp.arange(n)[None, :] // group
    return (r == c).astype(BF16)


def _head_expand():
    r = jnp.arange(LANES)[:, None]
    c = jnp.arange(GROUP_W)[None, :]
    return (r == c // C_HEADDIM).astype(BF16)


HEAD_W = 128
V_ROWS = 80
QK_SCALE = (D_NOPE + D_ROPE) ** -0.5
LOG2E = 1.4426950408889634


def _rope_body(posc_ref, posr_ref, fr_ref, frc_ref, cs_ref, sn_ref, cst_ref, snt_ref):
    ang = posc_ref[...] * fr_ref[...]
    cs_ref[...] = jnp.cos(ang)
    sn_ref[...] = jnp.sin(ang)
    angt = frc_ref[...] * posr_ref[...]
    cst_ref[...] = jnp.cos(angt)
    snt_ref[...] = jnp.sin(angt)


def _rope_tables(posc, posr, fr, frc, tm):
    t = posc.shape[0]
    rows = pl.BlockSpec((tm, LANES), lambda i: (i, 0))
    cols = pl.BlockSpec((D_ROPE, tm), lambda i: (0, i))
    return pl.pallas_call(
        _rope_body,
        grid=(t // tm,),
        in_specs=[pl.BlockSpec((tm, 1), lambda i: (i, 0)), pl.BlockSpec((1, tm), lambda i: (0, i)),
                  pl.BlockSpec(fr.shape, lambda i: (0, 0)), pl.BlockSpec(frc.shape, lambda i: (0, 0))],
        out_specs=[rows, rows, cols, cols],
        out_shape=[jax.ShapeDtypeStruct((t, LANES), F32)] * 2 + [jax.ShapeDtypeStruct((D_ROPE, t), F32)] * 2,
        compiler_params=_params("parallel"),
        name="rope_tables",
    )(posc, posr, fr, frc)


def _d_prep_body(pd_ref, pm_ref, cs_ref, sn_ref, cst_ref, snt_ref, qag_ref, wqt_ref, wqrt_ref, kvag_ref,
                 wk_ref, wvt_ref, qt_ref, k_ref, vt_ref):
    sc = QK_SCALE * LOG2E
    hq = _rms(pd_ref[:, 256:512], qag_ref[...]).astype(BF16)
    q0 = lax.dot_general(wqt_ref[...], hq, NT, preferred_element_type=F32)
    q1 = lax.dot_general(wqrt_ref[...], hq, NT, preferred_element_type=F32)
    cst = cst_ref[...]
    snt = snt_ref[...]
    for h in range(D_HEADS):
        r0 = h * HEAD_W
        qt_ref[h, 0:D_NOPE, :] = (q0[r0:r0 + D_NOPE] * sc).astype(BF16)
        rope = q0[r0 + D_NOPE:r0 + D_NOPE + D_ROPE] * cst + q1[h * D_ROPE:(h + 1) * D_ROPE] * snt
        qt_ref[h, D_NOPE:D_NOPE + D_ROPE, :] = (rope * sc).astype(BF16)
        qt_ref[h, D_NOPE + D_ROPE:HEAD_W, :] = jnp.zeros((HEAD_W - D_NOPE - D_ROPE, q0.shape[1]), BF16)

    hkv = _rms(pd_ref[:, 512:640], kvag_ref[...]).astype(BF16)
    k0 = jnp.dot(hkv, wk_ref[...], preferred_element_type=F32)
    cs = cs_ref[...]
    sn = sn_ref[...]
    lane = lax.broadcasted_iota(jnp.int32, cs.shape, 1)
    rope_lane = (lane >= D_NOPE) & (lane < D_NOPE + D_ROPE)
    kr = jnp.where(rope_lane, pm_ref[:, 0:LANES] * cs, 0.0) + pm_ref[:, LANES:2 * LANES] * sn
    k_ref[...] = (k0 + jnp.concatenate([kr] * D_HEADS, axis=1)).astype(BF16)

    vt = lax.dot_general(wvt_ref[...], hkv, NT, preferred_element_type=F32)
    pad = V_ROWS - D_VDIM
    ones_row = jnp.where(lax.broadcasted_iota(jnp.int32, (pad, vt.shape[1]), 0) == 0, 1.0, 0.0).astype(BF16)
    for h in range(D_HEADS):
        vt_ref[h, 0:D_VDIM, :] = vt[h * D_VDIM:(h + 1) * D_VDIM].astype(BF16)
        vt_ref[h, D_VDIM:V_ROWS, :] = ones_row


def _d_prep(pd, pm, rope, qag, wqt, wqrt, kvag, wk, wvt, l, batch, seq, tm):
    t = pd.shape[0]
    nseq = seq // tm
    cs, sn, cst, snt = rope
    rows = pl.BlockSpec((tm, LANES), lambda b, i: (b * nseq + i, 0))
    cols = pl.BlockSpec((D_ROPE, tm), lambda b, i: (0, b * nseq + i))
    return pl.pallas_call(
        _d_prep_body,
        grid=(batch, nseq),
        in_specs=[pl.BlockSpec((tm, W_PD), lambda b, i: (b * nseq + i, 0)),
                  pl.BlockSpec((tm, W_PM), lambda b, i: (b * nseq + i, 0)),
                  rows, rows, cols, cols,
                  _layer_spec(qag, l), _layer_spec(wqt, l), _layer_spec(wqrt, l),
                  _layer_spec(kvag, l), _layer_spec(wk, l), _layer_spec(wvt, l)],
        out_specs=[pl.BlockSpec((None, D_HEADS, HEAD_W, tm), lambda b, i: (b, 0, 0, i)),
                   pl.BlockSpec((tm, D_HEADS * HEAD_W), lambda b, i: (b * nseq + i, 0)),
                   pl.BlockSpec((None, D_HEADS, None, V_ROWS, tm), lambda b, i: (b, 0, i, 0, 0))],
        out_shape=[jax.ShapeDtypeStruct((batch, D_HEADS, HEAD_W, seq), BF16),
                   jax.ShapeDtypeStruct((t, D_HEADS * HEAD_W), BF16),
                   jax.ShapeDtypeStruct((batch, D_HEADS, nseq, V_ROWS, tm), BF16)],
        compiler_params=_params("parallel", "parallel"),
        name="mla_prep",
    )(pd, pm, cs, sn, cst, snt, qag, wqt, wqrt, kvag, wk, wvt)


ATT_HEADS_PER_STEP = 4


def _attn_body(qt_ref, k_ref, vt_ref, o_ref, acc, mrow, sa, sb, *, tq):
    i = pl.program_id(2)
    key_chunk = lax.broadcasted_iota(jnp.int32, (tq, tq), 0) // CHUNK
    qry_chunk = lax.broadcasted_iota(jnp.int32, (tq, tq), 1) // CHUNK
    visible = key_chunk <= qry_chunk
    heads = range(ATT_HEADS_PER_STEP)

    def scores(hh, kt):
        k0 = pl.multiple_of(kt * tq, tq)
        return jnp.dot(k_ref[pl.ds(k0, tq), hh * HEAD_W:(hh + 1) * HEAD_W], qt_ref[hh],
                       preferred_element_type=F32)

    def key_tile(u):
        return jnp.where(u == 0, i, u - 1)

    def fold(hh, st, u):
        m_old = mrow[hh]
        m_new = jnp.maximum(m_old, jnp.max(st, axis=0, keepdims=True))
        pt = jnp.exp2(st - m_new).astype(BF16)
        acc[hh] = (jnp.exp2(m_old - m_new) * acc[hh]
                   + jnp.dot(vt_ref[hh, key_tile(u)], pt, preferred_element_type=F32))
        mrow[hh] = m_new

    for hh in heads:
        acc[hh] = jnp.zeros(acc.shape[1:], F32)
        mrow[hh] = jnp.full(mrow.shape[1:], MASK_VALUE, F32)
        sa[hh] = jnp.where(visible, scores(hh, i), MASK_VALUE)

    def two_tiles(u):
        for hh in heads:
            sb[hh] = scores(hh, u)
            fold(hh, sa[hh], u)
        for hh in heads:
            sa[hh] = scores(hh, u + 1)
            fold(hh, sb[hh], u + 1)

    def quad_trip(p, carry):
        two_tiles(4 * p)
        two_tiles(4 * p + 2)
        return carry

    quads = lax.shift_right_logical(i, 2)
    lax.fori_loop(0, quads, quad_trip, 0)
    u1 = 4 * quads

    def pair_trip(p, carry):
        two_tiles(u1 + 2 * p)
        return carry

    pairs = lax.shift_right_logical(i - u1, 1)
    lax.fori_loop(0, pairs, pair_trip, 0)
    u0 = u1 + 2 * pairs
    left = i + 1 - u0

    @pl.when(left == 2)
    def _():
        for hh in heads:
            sb[hh] = scores(hh, u0)
            fold(hh, sa[hh], u0)
        for hh in heads:
            fold(hh, sb[hh], u0 + 1)

    @pl.when(left == 1)
    def _():
        for hh in heads:
            fold(hh, sa[hh], u0)

    outs = []
    for hh in heads:
        a = acc[hh]
        outs.append(a[0:D_VDIM, :] / a[D_VDIM:D_VDIM + 1, :])
    o_ref[...] = jnp.concatenate(outs, axis=0).T.astype(o_ref.dtype)


def _attention(qt, k, vt, batch, seq, tq):
    t = k.shape[0]
    nq = seq // tq
    g = ATT_HEADS_PER_STEP
    return pl.pallas_call(
        functools.partial(_attn_body, tq=tq),
        grid=(batch, D_HEADS // g, nq),
        in_specs=[pl.BlockSpec((None, g, HEAD_W, tq), lambda b, h, i: (b, h, 0, i)),
                  pl.BlockSpec((seq, g * HEAD_W), lambda b, h, i: (b, h), pipeline_mode=pl.Buffered(1)),
                  pl.BlockSpec((None, g, nq, V_ROWS, tq), lambda b, h, i: (b, h, 0, 0, 0),
                               pipeline_mode=pl.Buffered(1))],
        out_specs=pl.BlockSpec((tq, g * D_VDIM), lambda b, h, i: (b * nq + i, h)),
        out_shape=jax.ShapeDtypeStruct((t, D_HEADS * D_VDIM), BF16),
        scratch_shapes=[pltpu.VMEM((g, V_ROWS, tq), F32), pltpu.VMEM((g, 1, tq), F32),
                        pltpu.VMEM((g, tq, tq), F32), pltpu.VMEM((g, tq, tq), F32)],
        compiler_params=_params("arbitrary", "arbitrary", "arbitrary"),
        name="mla_attention",
    )(qt, k, vt)


def _out_body(ya_ref, yb_ref, yc_ref, od_ref, gd_ref, x_ref, w_ref, pg_ref, o_ref):
    yd = (od_ref[...].astype(F32) * _silu(gd_ref[...])).astype(BF16)
    y = None
    for n, part in enumerate((ya_ref[...], yb_ref[...], yc_ref[...], yd)):
        t = jnp.dot(part, w_ref[n * GROUP_W:(n + 1) * GROUP_W, :], preferred_element_type=F32)
        y = t if y is None else y + t
    o_ref[...] = x_ref[...] + _rms(y, pg_ref[...])


def _outproj(ya, yb, yc, od, pd, x, w2, post_g, l, tm):
    t = x.shape[0]
    part = pl.BlockSpec((tm, GROUP_W), lambda i: (i, 0))
    return pl.pallas_call(
        _out_body,
        grid=(t // tm,),
        in_specs=[part, part, part, part, part,
                  pl.BlockSpec((tm, D_MODEL), lambda i: (i, 0)),
                  _layer_spec(w2, l), _layer_spec(post_g, l)],
        out_specs=pl.BlockSpec((tm, D_MODEL), lambda i: (i, 0)),
        out_shape=jax.ShapeDtypeStruct((t, D_MODEL), F32),
        compiler_params=_params("parallel"),
        name="outproj",
    )(ya, yb, yc, od, pd, x, w2, post_g)


def _out_in_body(ya_ref, yb_ref, yc_ref, od_ref, gd_ref, x_ref, w2_ref, pg_ref, g1_ref, w1_ref, o_ref,
                 oa, ob, oc, od, om, hscr):
    @pl.when(pl.program_id(0) == 0)
    def _():
        hscr[...] = jnp.zeros_like(hscr)

    h_prev = hscr[...]
    off = 0
    for o in (oa, ob, oc, od, om):
        n = o.shape[-1]
        o[...] = jnp.dot(h_prev, w1_ref[:, off:off + n], preferred_element_type=F32)
        off += n
    _out_body(ya_ref, yb_ref, yc_ref, od_ref, gd_ref, x_ref, w2_ref, pg_ref, o_ref)
    hscr[...] = _rms(o_ref[...], g1_ref[...]).astype(BF16)


def _outproj_inproj(ya, yb, yc, od, pd, x, w2, post_g, pre_g, w1, l, tm):
    t = x.shape[0]
    nt = t // tm
    cur = lambda i: (jnp.minimum(i, nt - 1), 0)
    prev = lambda i: (jnp.maximum(i - 1, 0), 0)
    part = pl.BlockSpec((tm, GROUP_W), cur)
    widths = (W_PA, W_PB, W_PC, W_PD, W_PM)
    res = pl.pallas_call(
        _out_in_body,
        grid=(nt + 1,),
        in_specs=[part, part, part, part, part,
                  pl.BlockSpec((tm, D_MODEL), cur),
                  _layer_spec(w2, l), _layer_spec(post_g, l),
                  _layer_spec(pre_g, l + 1), _layer_spec(w1, l + 1)],
        out_specs=[pl.BlockSpec((tm, D_MODEL), cur)] + [pl.BlockSpec((tm, n), prev) for n in widths],
        out_shape=[jax.ShapeDtypeStruct((t, D_MODEL), F32)]
                  + [jax.ShapeDtypeStruct((t, n), F32) for n in widths],
        scratch_shapes=[pltpu.VMEM((tm, D_MODEL), BF16)],
        compiler_params=_params("arbitrary"),
        name="outproj_inproj",
    )(ya, yb, yc, od, pd, x, w2, post_g, pre_g, w1)
    return res[0], res[1:]


def _rot_cols(w):
    half = D_ROPE // 2
    return jnp.concatenate([-w[..., half:], w[..., :half]], axis=-1)


def _prep_w_in(w_in):
    w_in = w_in.astype(BF16)
    d = w_in.shape[0]
    z = lambda n: jnp.zeros((d, D_MODEL, n), w_in.dtype)
    c = lambda a, n: w_in[:, :, a:a + n]
    kr = c(3204, D_ROPE)
    cols = [c(0, 768),
            c(768, 1024),
            c(1792, 768), c(2564, 256),
            c(3236, 256), c(2820, 256), c(3076, 128),
            c(2560, 4), z(60), kr, z(32),
            z(64), _rot_cols(kr), z(32)]
    return jnp.concatenate(cols, axis=-1)


def _prep_q(qb_w):
    d = qb_w.shape[0]
    z = lambda n: jnp.zeros((d, D_Q_RANK, n), qb_w.dtype)
    plain, rot = [], []
    for h in range(D_HEADS):
        o = h * (D_NOPE + D_ROPE)
        nope, rope = qb_w[:, :, o:o + D_NOPE], qb_w[:, :, o + D_NOPE:o + D_NOPE + D_ROPE]
        plain += [nope, rope, z(HEAD_W - D_NOPE - D_ROPE)]
        rot += [_rot_cols(rope)]
    t = lambda cols: jnp.swapaxes(jnp.concatenate(cols, axis=-1), 1, 2).astype(BF16)
    return t(plain), t(rot)


def _prep_kv(kvb_w):
    d = kvb_w.shape[0]
    z = lambda n: jnp.zeros((d, D_KV_RANK, n), kvb_w.dtype)
    ks, vs = [], []
    for h in range(D_HEADS):
        o = h * (D_NOPE + D_VDIM)
        ks += [kvb_w[:, :, o:o + D_NOPE], z(HEAD_W - D_NOPE)]
        vs += [kvb_w[:, :, o + D_NOPE:o + D_NOPE + D_VDIM]]
    return (jnp.concatenate(ks, axis=-1).astype(BF16),
            jnp.swapaxes(jnp.concatenate(vs, axis=-1), 1, 2).astype(BF16))


def _row(p, width=None):
    if width is not None and width > p.shape[-1]:
        p = jnp.pad(p, ((0, 0), (0, width - p.shape[-1])))
    return p[:, None, :]


def _pick_tile(n, want):
    t = min(want, n)
    while n % t:
        t //= 2
    return t


@jax.jit
def _forward(x, positions, pre_norm_g, post_norm_g, w_in, w_out, a_dw_w, a_dw_b, a_ln_g, a_ln_b,
             a_pw_w, a_pw_b, b_lb_logits, b_norm_g, c_conv_w, c_conv_b, c_dt_bias, c_a_log, c_d,
             c_norm_g, d_qa_g, d_qb_w, d_kva_g, d_kvb_w):
    batch, seq, _ = x.shape
    depth = w_in.shape[0]
    t = batch * seq
    tm = _pick_tile(seq, 512)
    tb = _pick_tile(seq, 256)

    w1 = _prep_w_in(w_in)
    w2 = w_out.astype(BF16)
    wqt, wqrt = _prep_q(d_qb_w)
    wk, wvt = _prep_kv(d_kvb_w)
    dww = jnp.pad(a_dw_w, ((0, 0), (0, 32 - A_KERNEL), (0, 0)))
    pww = a_pw_w.astype(BF16)
    dsk = _row(jnp.repeat(c_d, C_HEADDIM, axis=-1))
    alog_e = _row(jnp.repeat(c_a_log, C_HEADDIM, axis=-1))
    inv_freq = ROPE_THETA ** (-jnp.arange(0, D_ROPE, 2, dtype=F32) / D_ROPE)
    fr = jnp.concatenate([jnp.zeros((D_NOPE,), F32), inv_freq, inv_freq,
                          jnp.zeros((HEAD_W - D_NOPE - D_ROPE,), F32)])[None, :]
    frc = jnp.concatenate([inv_freq, inv_freq])[:, None]
    posc = positions.astype(F32).reshape(t, 1)
    posr = positions.astype(F32).reshape(1, t)

    rope = _rope_tables(posc, posr, fr, frc, tm)

    xf = x.reshape(t, D_MODEL)
    pa, pb, pc, pd, pm = _inproj(xf, _row(pre_norm_g), w1, 0, tm)
    for l in range(depth):
        ya = _branch_a(pa, dww, _row(a_dw_b), _row(a_ln_g), _row(a_ln_b), pww, _row(a_pw_b), l, seq, tm)
        yb = _branch_b(pb, b_lb_logits, _row(b_norm_g), l, batch, seq, tb)
        yc = _branch_c(pc, pm, c_conv_w, _row(c_conv_b), _row(c_dt_bias, LANES), alog_e,
                       dsk, _row(c_norm_g), l, batch, seq, tm)
        qt, k, vt = _d_prep(pd, pm, rope, _row(d_qa_g), wqt, wqrt, _row(d_kva_g), wk, wvt, l, batch, seq, tm)
        od = _attention(qt, k, vt, batch, seq, tm)
        if l + 1 < depth:
            xf, (pa, pb, pc, pd, pm) = _outproj_inproj(ya, yb, yc, od, pd, xf, w2, _row(post_norm_g),
                                                       _row(pre_norm_g), w1, l, tm)
        else:
            xf = _outproj(ya, yb, yc, od, pd, xf, w2, _row(post_norm_g), l, tm)
    return xf.reshape(batch, seq, D_MODEL)


def kernel(x, positions, pre_norm_g, post_norm_g, w_in, w_out, a_dw_w, a_dw_b, a_ln_g, a_ln_b, a_pw_w, a_pw_b, b_lb_logits, b_norm_g, c_conv_w, c_conv_b, c_dt_bias, c_a_log, c_d, c_norm_g, d_qa_g, d_qb_w, d_kva_g, d_kvb_w):
    return _forward(x, positions, pre_norm_g, post_norm_g, w_in, w_out, a_dw_w, a_dw_b, a_ln_g, a_ln_b,
                    a_pw_w, a_pw_b, b_lb_logits, b_norm_g, c_conv_w, c_conv_b, c_dt_bias, c_a_log, c_d,
                    c_norm_g, d_qa_g, d_qb_w, d_kva_g, d_kvb_w)
```

```python
import functools
import math

import jax
import jax.numpy as jnp
from jax import lax
from jax.experimental import pallas as pl
from jax.experimental.pallas import tpu as pltpu

F32 = jnp.float32
BF16 = jnp.bfloat16

D_MODEL = 1024
CHUNK = 64
GROUP_W = 256
NORM_EPS = 1e-6
LN_EPS = 1e-5
MASK_VALUE = -1e30
GATE_FLOOR = 1e-20
A_KERNEL = 31
B_HEADS = 4
B_DK = 64
C_HEADS = 4
C_HEADDIM = 64
C_GROUPS = 2
C_STATE = 128
C_CONV = 4
C_XBC = 768
D_HEADS = 4
D_Q_RANK = 256
D_KV_RANK = 128
D_NOPE = 64
D_ROPE = 32
D_VDIM = 64
ROPE_THETA = 10000.0

LANES = 128
SUB = 16
WIDE = 32
FACTOR_SAFE_LOG2 = 100.0
VMEM_LIMIT = 48 * 1024 * 1024

W_PA, W_PB, W_PC, W_PD, W_PM = 768, 1024, 1024, 640, 256
N_PROJ = W_PA + W_PB + W_PC + W_PD + W_PM

NT = (((1,), (1,)), ((), ()))
TN = (((0,), (0,)), ((), ()))


def _params(*sem):
    return pltpu.CompilerParams(dimension_semantics=sem, vmem_limit_bytes=VMEM_LIMIT)


def _layer_spec(arr, l):
    shape = arr.shape[1:]
    nd = len(shape)
    return pl.BlockSpec((None,) + shape, lambda *_: (l,) + (0,) * nd)


def _split3(x):
    h1 = x.astype(BF16)
    r1 = x - h1.astype(F32)
    h2 = r1.astype(BF16)
    h3 = (r1 - h2.astype(F32)).astype(BF16)
    return h1, h2, h3


def _dot_sel(sel, x, dims=None, sel_on_right=False):
    if dims is None:
        dims = (((1,), (0,)), ((), ()))
    out = None
    for piece in _split3(x):
        a, b = (piece, sel) if sel_on_right else (sel, piece)
        t = lax.dot_general(a, b, dims, preferred_element_type=F32)
        out = t if out is None else out + t
    return out


def _silu(x):
    return x * jax.nn.sigmoid(x)


def _rms(x, g):
    ms = jnp.mean(x * x, axis=-1, keepdims=True)
    return x * lax.rsqrt(ms + NORM_EPS) * g


def _inproj_body(x_ref, g_ref, w_ref, *outs):
    h = _rms(x_ref[...], g_ref[...]).astype(BF16)
    off = 0
    for o in outs:
        n = o.shape[-1]
        o[...] = jnp.dot(h, w_ref[:, off:off + n], preferred_element_type=F32).astype(o.dtype)
        off += n


def _inproj(x, pre_g, w1, l, tm):
    t = x.shape[0]
    widths = (W_PA, W_PB, W_PC, W_PD, W_PM)
    return pl.pallas_call(
        _inproj_body,
        grid=(t // tm,),
        in_specs=[pl.BlockSpec((tm, D_MODEL), lambda i: (i, 0)),
                  _layer_spec(pre_g, l), _layer_spec(w1, l)],
        out_specs=[pl.BlockSpec((tm, n), lambda i: (i, 0)) for n in widths],
        out_shape=[jax.ShapeDtypeStruct((t, n), F32) for n in widths],
        compiler_params=_params("parallel"),
        name="inproj",
    )(x, pre_g, w1)


A_HALO = 32


SUBLANES = 8


def _a_body(pa_ref, ph_ref, dww_ref, dwb_ref, lng_ref, lnb_ref, pww_ref, pwb_ref, o_ref,
            hscr, gscr, cscr, *, ts, nseq):
    first = (pl.program_id(0) % nseq) == 0
    h = pa_ref[:, 0:256] * jax.nn.sigmoid(pa_ref[:, 256:512])
    hh = ph_ref[:, 0:256] * jax.nn.sigmoid(ph_ref[:, 256:512])
    hscr[0:A_HALO, :] = jnp.where(first, 0.0, hh)
    hscr[A_HALO:A_HALO + ts, :] = h
    w = dww_ref[...]
    n8 = ts // SUBLANES + 1
    blk = SUBLANES * max(d for d in range(1, 17) if n8 % d == 0)
    for b in range(SUBLANES):
        taps = [(a, A_KERNEL - 1 - (SUBLANES * a + b)) for a in range(A_HALO // SUBLANES)
                if SUBLANES * a + b < A_KERNEL]
        for g0 in range(0, ts + SUBLANES, blk):
            acc = None
            for a, j in taps:
                s0 = g0 + A_HALO - SUBLANES - SUBLANES * a
                term = w[j:j + 1, :] * hscr[s0:s0 + blk, :]
                acc = term if acc is None else acc + term
            gscr[b, g0:g0 + blk, :] = acc
    rows = 64
    for r in range(ts // rows):
        acc = jnp.broadcast_to(dwb_ref[...], (rows, GROUP_W))
        for b in range(SUBLANES):
            s0 = SUBLANES + r * rows - b
            acc = acc + gscr[b, s0:s0 + rows, :]
        cscr[r * rows:(r + 1) * rows, :] = acc
    c = cscr[...]
    mu = jnp.mean(c, axis=-1, keepdims=True)
    d = c - mu
    var = jnp.mean(d * d, axis=-1, keepdims=True)
    y = d * lax.rsqrt(var + LN_EPS) * lng_ref[...] + lnb_ref[...]
    y = _silu(y).astype(BF16)
    y = jnp.dot(y, pww_ref[...], preferred_element_type=F32) + pwb_ref[...]
    o_ref[...] = (y * _silu(pa_ref[:, 512:768])).astype(o_ref.dtype)


def _branch_a(pa, dww, dwb, lng, lnb, pww, pwb, l, seq, ts):
    t = pa.shape[0]
    nseq = seq // ts
    hb = ts // A_HALO
    return pl.pallas_call(
        functools.partial(_a_body, ts=ts, nseq=nseq),
        grid=(t // ts,),
        in_specs=[pl.BlockSpec((ts, W_PA), lambda i: (i, 0)),
                  pl.BlockSpec((A_HALO, W_PA), lambda i: (jnp.maximum(i * hb - 1, 0), 0)),
                  _layer_spec(dww, l), _layer_spec(dwb, l), _layer_spec(lng, l),
                  _layer_spec(lnb, l), _layer_spec(pww, l), _layer_spec(pwb, l)],
        out_specs=pl.BlockSpec((ts, GROUP_W), lambda i: (i, 0)),
        out_shape=jax.ShapeDtypeStruct((t, GROUP_W), BF16),
        scratch_shapes=[pltpu.VMEM((A_HALO + ts, GROUP_W), F32),
                        pltpu.VMEM((SUBLANES, ts + SUBLANES, GROUP_W), F32),
                        pltpu.VMEM((ts, GROUP_W), F32)],
        compiler_params=_params("arbitrary"),
        name="branch_a",
    )(pa, pa, dww, dwb, lng, lnb, pww, pwb)


def _group_tril(n, group):
    r = jnp.arange(n)[:, None]
    c = jnp.arange(n)[None, :]
    return ((r // group == c // group) & (c <= r)).astype(BF16)


def _b_body(pb_ref, lbl_ref, ng_ref, tril_ref, trilsub_ref, o_ref, st_ref, kscr, bscr, oscr, *, l, tb):
    @pl.when(pl.program_id(1) == 0)
    def _():
        st_ref[...] = jnp.zeros_like(st_ref)

    lg = lbl_ref[...]
    e = jnp.exp(lg - jnp.max(lg, axis=0, keepdims=True))
    p = e / jnp.sum(e, axis=0, keepdims=True)
    lb = jnp.zeros((1, GROUP_W), F32)
    for m in range(1, l + 1):
        lb = lb + p[m:m + 1, :]

    zf = pb_ref[:, 256:512]
    fg = lb + (1.0 - lb) * jax.nn.sigmoid(zf)
    logf = jnp.log(jnp.maximum(fg, GATE_FLOOR))
    kscr[...] = (1.0 - lb) * jax.nn.sigmoid(-zf)
    bscr[...] = _dot_sel(tril_ref[...], logf) * LOG2E

    hr = lax.broadcasted_iota(jnp.int32, (GROUP_W, GROUP_W), 0) // B_DK
    hc = lax.broadcasted_iota(jnp.int32, (GROUP_W, GROUP_W), 1) // B_DK
    same_head = hr == hc
    bd = jnp.where(same_head, 1.0, 0.0).astype(BF16)
    half = SUB // 2
    tio = lax.broadcasted_iota(jnp.int32, (SUB, GROUP_W), 0)
    tio_hi = lax.broadcasted_iota(jnp.int32, (half, GROUP_W), 0) + half
    zero_lo = jnp.zeros((half, GROUP_W), F32)

    group = 8
    gr = group * SUB
    lane_head = lax.broadcasted_iota(jnp.int32, (gr, GROUP_W), 1) // B_DK

    def stack_heads(x):
        return jnp.concatenate([jnp.where(lane_head == h, x, 0.0) for h in range(B_HEADS)], axis=0).astype(BF16)

    prow = lax.broadcasted_iota(jnp.int32, (gr, B_HEADS * gr), 0)
    pcol = lax.broadcasted_iota(jnp.int32, (gr, B_HEADS * gr), 1) % gr
    same_block_causal = (prow // WIDE == pcol // WIDE) & (prow >= pcol)

    def intra_factored(qg, kg, vg, bg):
        qt = (qg * jnp.exp2(bg)).astype(BF16)
        p = lax.dot_general(qt, stack_heads(kg * jnp.exp2(-bg)), NT, preferred_element_type=F32)
        p = jnp.where(same_block_causal, p, 0.0).astype(BF16)
        return jnp.dot(p, stack_heads(vg), preferred_element_type=F32)

    def intra_pairwise_group(qg, kg, vg, bg):
        parts = [slice(n * SUB, (n + 1) * SUB) for n in range(group)]
        return jnp.concatenate([intra_pairwise(qg[p], kg[p], vg[p], bg[p]) for p in parts], axis=0)

    def intra_pairwise(qj, kj, vj, bj):
        ws = []
        for s in range(SUB):
            if s < half:
                w = jnp.where(tio >= s, qj * kj[s:s + 1, :] * jnp.exp2(bj - bj[s:s + 1, :]), 0.0)
            else:
                w = jnp.where(tio_hi >= s, qj[half:] * kj[s:s + 1, :] * jnp.exp2(bj[half:] - bj[s:s + 1, :]), 0.0)
                w = jnp.concatenate([zero_lo, w], axis=0)
            ws.append(w.astype(BF16))
        m = jnp.dot(jnp.concatenate(ws, axis=0), bd, preferred_element_type=F32)
        o_lo = jnp.zeros((half, GROUP_W), F32)
        o_hi = jnp.zeros((half, GROUP_W), F32)
        for s in range(SUB):
            vrow = vj[s:s + 1, :]
            if s < half:
                o_lo = o_lo + m[s * SUB:s * SUB + half, :] * vrow
            o_hi = o_hi + m[s * SUB + half:(s + 1) * SUB, :] * vrow
        return jnp.concatenate([o_lo, o_hi], axis=0)

    def state_step(st, qj, kj, vj, bj):
        qt = (qj * jnp.exp2(bj)).astype(BF16)
        o = lax.dot_general(qt, st.astype(BF16), NT, preferred_element_type=F32)
        nr = bj.shape[0]
        blast = bj[nr - 1:nr, :]
        kh = (kj * jnp.exp2(blast - bj)).astype(BF16)
        u = lax.dot_general(vj.astype(BF16), kh, TN, preferred_element_type=F32)
        return o, st * jnp.exp2(blast) + jnp.where(same_head, u, 0.0)

    def run(intra_group, b_ref, block):
        def trip(jj, carry):
            r0 = pl.multiple_of(jj * gr, gr)
            rows = pl.ds(r0, gr)
            qg, vg = pb_ref[rows, 0:256], pb_ref[rows, 512:768]
            kg, bg = kscr[rows, :], b_ref[rows, :]
            st = st_ref[...]
            o_intra = intra_group(qg, kg, vg, bg)
            outs = []
            for n in range(gr // block):
                p = slice(n * block, (n + 1) * block)
                o_inter, st = state_step(st, qg[p], kg[p], vg[p], bg[p])
                outs.append(o_inter)
            oscr[rows, :] = o_intra + jnp.concatenate(outs, axis=0)
            st_ref[...] = st
            return carry

        lax.fori_loop(0, tb // gr, trip, 0)

    mild = jnp.min(bscr[...]) > -FACTOR_SAFE_LOG2

    @pl.when(mild)
    def _():
        run(intra_factored, bscr, WIDE)

    @pl.when(jnp.logical_not(mild))
    def _():
        bscr[...] = _dot_sel(trilsub_ref[...], logf) * LOG2E
        run(intra_pairwise_group, bscr, SUB)

    o = oscr[...]
    ms = _dot_sel(bd, o * o, sel_on_right=True) * (1.0 / B_DK)
    y = o * lax.rsqrt(ms + NORM_EPS) * ng_ref[...]
    o_ref[...] = (y * _silu(pb_ref[:, 768:1024])).astype(o_ref.dtype)


def _branch_b(pb, lbl, ng, l, batch, seq, tb):
    t = pb.shape[0]
    nseq = seq // tb
    return pl.pallas_call(
        functools.partial(_b_body, l=l, tb=tb),
        grid=(batch, nseq),
        in_specs=[pl.BlockSpec((tb, W_PB), lambda b, j: (b * nseq + j, 0)),
                  pl.BlockSpec(lbl.shape, lambda b, j: (0, 0)),
                  _layer_spec(ng, l),
                  pl.BlockSpec((tb, tb), lambda b, j: (0, 0)),
                  pl.BlockSpec((tb, tb), lambda b, j: (0, 0))],
        out_specs=pl.BlockSpec((tb, GROUP_W), lambda b, j: (b * nseq + j, 0)),
        out_shape=jax.ShapeDtypeStruct((t, GROUP_W), BF16),
        scratch_shapes=[pltpu.VMEM((GROUP_W, GROUP_W), F32),
                        pltpu.VMEM((tb, GROUP_W), F32),
                        pltpu.VMEM((tb, GROUP_W), F32),
                        pltpu.VMEM((tb, GROUP_W), F32)],
        compiler_params=_params("arbitrary", "arbitrary"),
        name="branch_b",
    )(pb, lbl, ng, _group_tril(tb, WIDE), _group_tril(tb, SUB))


C_HALO = 8


def _c_body(pc_ref, ph_ref, pm_ref, cw_ref, cb_ref, dtb_ref, alog_ref, dsk_ref, ng_ref, tril_ref, same_ref,
            exp_ref, o_ref, h_ref, xscr, cscr, aescr, lmscr, xdscr, yscr, *, tc):
    first = pl.program_id(1) == 0

    @pl.when(first)
    def _():
        h_ref[...] = jnp.zeros_like(h_ref)

    xscr[0:C_HALO, :] = jnp.where(first, 0.0, ph_ref[:, 0:C_XBC])
    xscr[C_HALO:C_HALO + tc, :] = pc_ref[:, 0:C_XBC]
    rows = 64
    base = C_HALO - (C_CONV - 1)
    for lb in range(C_XBC // GROUP_W):
        ls = slice(lb * GROUP_W, (lb + 1) * GROUP_W)
        w = cw_ref[:, ls]
        bias = jnp.broadcast_to(cb_ref[:, ls], (rows, GROUP_W))
        for r in range(tc // rows):
            acc = bias
            for j in range(C_CONV):
                s0 = r * rows + base + j
                acc = acc + w[j:j + 1, :] * xscr[s0:s0 + rows, ls]
            cscr[r * rows:(r + 1) * rows, ls] = _silu(acc)

    dt = jnp.logaddexp(pm_ref[:, 0:LANES] + dtb_ref[...], 0.0)
    dte = _dot_sel(exp_ref[...], dt, sel_on_right=True)
    xdscr[...] = cscr[:, 0:GROUP_W] * dte
    a = dte * (-jnp.exp(alog_ref[...]))
    tb = tril_ref.shape[0]
    tpos = lax.broadcasted_iota(jnp.int32, (tb, GROUP_W), 0) % CHUNK
    spos = lax.broadcasted_iota(jnp.int32, (tb, GROUP_W), 1) % C_HEADDIM
    for r in range(tc // tb):
        rows = slice(r * tb, (r + 1) * tb)
        acs = _dot_sel(tril_ref[...], a[rows, :])
        aescr[rows, :] = acs
        acs_t = _dot_sel(same_ref[...], jnp.where(tpos == spos, acs, 0.0))
        lmscr[rows, :] = jnp.where(tpos >= spos, jnp.exp(jnp.minimum(acs - acs_t, 0.0)), 0.0)

    e = C_HEADS // C_GROUPS
    gw = e * C_HEADDIM
    first_head = lax.broadcasted_iota(jnp.int32, (CHUNK, gw), 1) < C_HEADDIM
    dsk = dsk_ref[...]

    def chunk(c, carry):
        r0 = pl.multiple_of(c * CHUNK, CHUNK)
        rs = pl.ds(r0, CHUNK)
        ae = aescr[rs, :]
        alast = ae[CHUNK - 1:CHUNK, :]
        xs = cscr[rs, 0:GROUP_W]
        xdt = xdscr[rs, :]
        wx = xdt * jnp.exp(alast - ae)
        for g in range(C_GROUPS):
            ls = slice(g * gw, (g + 1) * gw)
            bmg = cscr[rs, GROUP_W + g * C_STATE:GROUP_W + (g + 1) * C_STATE].astype(BF16)
            c0 = GROUP_W + C_GROUPS * C_STATE + g * C_STATE
            cmg = cscr[rs, c0:c0 + C_STATE].astype(BF16)
            cb2 = lax.dot_general(cmg, jnp.concatenate([bmg, bmg], axis=0), NT, preferred_element_type=F32)
            xg = xdt[:, ls]
            x2 = jnp.concatenate([jnp.where(first_head, xg, 0.0), jnp.where(first_head, 0.0, xg)], axis=0)
            ydiag = jnp.dot((cb2 * lmscr[rs, ls]).astype(BF16), x2.astype(BF16), preferred_element_type=F32)
            ht = h_ref[g * C_STATE:(g + 1) * C_STATE, :]
            yoff = jnp.dot(cmg, ht.astype(BF16), preferred_element_type=F32) * jnp.exp(ae[:, ls])
            stt = lax.dot_general(bmg, wx[:, ls].astype(BF16), TN, preferred_element_type=F32)
            h_ref[g * C_STATE:(g + 1) * C_STATE, :] = ht * jnp.exp(alast[:, ls]) + stt
            yscr[rs, ls] = ydiag + yoff + xs[:, ls] * dsk[:, ls]
        return carry

    lax.fori_loop(0, tc // CHUNK, chunk, 0, unroll=2)

    yz = yscr[...] * _silu(pc_ref[:, C_XBC:C_XBC + GROUP_W])
    gw = e * C_HEADDIM
    for g in range(C_GROUPS):
        o_ref[:, g * gw:(g + 1) * gw] = _rms(yz[:, g * gw:(g + 1) * gw],
                                             ng_ref[:, g * gw:(g + 1) * gw]).astype(o_ref.dtype)


def _branch_c(pc, pm, cw, cb, dtb, alog, dsk, ng, l, batch, seq, tc):
    t = pc.shape[0]
    nseq = seq // tc
    hb = tc // C_HALO
    tt = min(tc, 256)
    return pl.pallas_call(
        functools.partial(_c_body, tc=tc),
        grid=(batch, nseq),
        in_specs=[pl.BlockSpec((tc, W_PC), lambda b, j: (b * nseq + j, 0)),
                  pl.BlockSpec((C_HALO, W_PC), lambda b, j: (jnp.maximum((b * nseq + j) * hb - 1, 0), 0)),
                  pl.BlockSpec((tc, W_PM), lambda b, j: (b * nseq + j, 0)),
                  _layer_spec(cw, l), _layer_spec(cb, l), _layer_spec(dtb, l),
                  _layer_spec(alog, l), _layer_spec(dsk, l), _layer_spec(ng, l),
                  pl.BlockSpec((tt, tt), lambda b, j: (0, 0)),
                  pl.BlockSpec((tt, tt), lambda b, j: (0, 0)),
                  pl.BlockSpec((LANES, GROUP_W), lambda b, j: (0, 0))],
        out_specs=pl.BlockSpec((tc, GROUP_W), lambda b, j: (b * nseq + j, 0)),
        out_shape=jax.ShapeDtypeStruct((t, GROUP_W), BF16),
        scratch_shapes=[pltpu.VMEM((C_GROUPS * C_STATE, C_HEADS // C_GROUPS * C_HEADDIM), F32),
                        pltpu.VMEM((C_HALO + tc, C_XBC), F32),
                        pltpu.VMEM((tc, C_XBC), F32),
                        pltpu.VMEM((tc, GROUP_W), F32),
                        pltpu.VMEM((tc, GROUP_W), F32),
                        pltpu.VMEM((tc, GROUP_W), F32),
                        pltpu.VMEM((tc, GROUP_W), F32)],
        compiler_params=_params("arbitrary", "arbitrary"),
        name="branch_c",
    )(pc, pc, pm, cw, cb, dtb, alog, dsk, ng, _group_tril(tt, CHUNK), _group_ones(tt, CHUNK), _head_expand())


def _group_ones(n, group):
    r = jnp.arange(n)[:, None] // group
    c = jnp.arange(n)[None, :] // group
    return (r == c).astype(BF16)


def _head_expand():
    r = jnp.arange(LANES)[:, None]
    c = jnp.arange(GROUP_W)[None, :]
    return (r == c // C_HEADDIM).astype(BF16)


HEAD_W = 128
V_ROWS = 80
QK_SCALE = (D_NOPE + D_ROPE) ** -0.5
LOG2E = 1.4426950408889634


def _rope_body(posc_ref, posr_ref, fr_ref, frc_ref, cs_ref, sn_ref, cst_ref, snt_ref):
    ang = posc_ref[...] * fr_ref[...]
    cs_ref[...] = jnp.cos(ang)
    sn_ref[...] = jnp.sin(ang)
    angt = frc_ref[...] * posr_ref[...]
    cst_ref[...] = jnp.cos(angt)
    snt_ref[...] = jnp.sin(angt)


def _rope_tables(posc, posr, fr, frc, tm):
    t = posc.shape[0]
    rows = pl.BlockSpec((tm, LANES), lambda i: (i, 0))
    cols = pl.BlockSpec((D_ROPE, tm), lambda i: (0, i))
    return pl.pallas_call(
        _rope_body,
        grid=(t // tm,),
        in_specs=[pl.BlockSpec((tm, 1), lambda i: (i, 0)), pl.BlockSpec((1, tm), lambda i: (0, i)),
                  pl.BlockSpec(fr.shape, lambda i: (0, 0)), pl.BlockSpec(frc.shape, lambda i: (0, 0))],
        out_specs=[rows, rows, cols, cols],
        out_shape=[jax.ShapeDtypeStruct((t, LANES), F32)] * 2 + [jax.ShapeDtypeStruct((D_ROPE, t), F32)] * 2,
        compiler_params=_params("parallel"),
        name="rope_tables",
    )(posc, posr, fr, frc)


def _d_prep_body(pd_ref, pm_ref, cs_ref, sn_ref, cst_ref, snt_ref, qag_ref, wqt_ref, wqrt_ref, kvag_ref,
                 wk_ref, wvt_ref, qt_ref, k_ref, vt_ref):
    sc = QK_SCALE * LOG2E
    hq = _rms(pd_ref[:, 256:512], qag_ref[...]).astype(BF16)
    q0 = lax.dot_general(wqt_ref[...], hq, NT, preferred_element_type=F32)
    q1 = lax.dot_general(wqrt_ref[...], hq, NT, preferred_element_type=F32)
    cst = cst_ref[...]
    snt = snt_ref[...]
    for h in range(D_HEADS):
        r0 = h * HEAD_W
        qt_ref[h, 0:D_NOPE, :] = (q0[r0:r0 + D_NOPE] * sc).astype(BF16)
        rope = q0[r0 + D_NOPE:r0 + D_NOPE + D_ROPE] * cst + q1[h * D_ROPE:(h + 1) * D_ROPE] * snt
        qt_ref[h, D_NOPE:D_NOPE + D_ROPE, :] = (rope * sc).astype(BF16)
        qt_ref[h, D_NOPE + D_ROPE:HEAD_W, :] = jnp.zeros((HEAD_W - D_NOPE - D_ROPE, q0.shape[1]), BF16)

    hkv = _rms(pd_ref[:, 512:640], kvag_ref[...]).astype(BF16)
    k0 = jnp.dot(hkv, wk_ref[...], preferred_element_type=F32)
    cs = cs_ref[...]
    sn = sn_ref[...]
    lane = lax.broadcasted_iota(jnp.int32, cs.shape, 1)
    rope_lane = (lane >= D_NOPE) & (lane < D_NOPE + D_ROPE)
    kr = jnp.where(rope_lane, pm_ref[:, 0:LANES] * cs, 0.0) + pm_ref[:, LANES:2 * LANES] * sn
    k_ref[...] = (k0 + jnp.concatenate([kr] * D_HEADS, axis=1)).astype(BF16)

    vt = lax.dot_general(wvt_ref[...], hkv, NT, preferred_element_type=F32)
    pad = V_ROWS - D_VDIM
    ones_row = jnp.where(lax.broadcasted_iota(jnp.int32, (pad, vt.shape[1]), 0) == 0, 1.0, 0.0).astype(BF16)
    for h in range(D_HEADS):
        vt_ref[h, 0:D_VDIM, :] = vt[h * D_VDIM:(h + 1) * D_VDIM].astype(BF16)
        vt_ref[h, D_VDIM:V_ROWS, :] = ones_row


def _d_prep(pd, pm, rope, qag, wqt, wqrt, kvag, wk, wvt, l, batch, seq, tm):
    t = pd.shape[0]
    nseq = seq // tm
    cs, sn, cst, snt = rope
    rows = pl.BlockSpec((tm, LANES), lambda b, i: (b * nseq + i, 0))
    cols = pl.BlockSpec((D_ROPE, tm), lambda b, i: (0, b * nseq + i))
    return pl.pallas_call(
        _d_prep_body,
        grid=(batch, nseq),
        in_specs=[pl.BlockSpec((tm, W_PD), lambda b, i: (b * nseq + i, 0)),
                  pl.BlockSpec((tm, W_PM), lambda b, i: (b * nseq + i, 0)),
                  rows, rows, cols, cols,
                  _layer_spec(qag, l), _layer_spec(wqt, l), _layer_spec(wqrt, l),
                  _layer_spec(kvag, l), _layer_spec(wk, l), _layer_spec(wvt, l)],
        out_specs=[pl.BlockSpec((None, D_HEADS, HEAD_W, tm), lambda b, i: (b, 0, 0, i)),
                   pl.BlockSpec((tm, D_HEADS * HEAD_W), lambda b, i: (b * nseq + i, 0)),
                   pl.BlockSpec((None, D_HEADS, None, V_ROWS, tm), lambda b, i: (b, 0, i, 0, 0))],
        out_shape=[jax.ShapeDtypeStruct((batch, D_HEADS, HEAD_W, seq), BF16),
                   jax.ShapeDtypeStruct((t, D_HEADS * HEAD_W), BF16),
                   jax.ShapeDtypeStruct((batch, D_HEADS, nseq, V_ROWS, tm), BF16)],
        compiler_params=_params("parallel", "parallel"),
        name="mla_prep",
    )(pd, pm, cs, sn, cst, snt, qag, wqt, wqrt, kvag, wk, wvt)


ATT_HEADS_PER_STEP = 4


def _attn_body(qt_ref, k_ref, vt_ref, o_ref, acc, mrow, sa, sb, *, tq):
    i = pl.program_id(2)
    key_chunk = lax.broadcasted_iota(jnp.int32, (tq, tq), 0) // CHUNK
    qry_chunk = lax.broadcasted_iota(jnp.int32, (tq, tq), 1) // CHUNK
    visible = key_chunk <= qry_chunk
    heads = range(ATT_HEADS_PER_STEP)

    def scores(hh, kt):
        k0 = pl.multiple_of(kt * tq, tq)
        return jnp.dot(k_ref[pl.ds(k0, tq), hh * HEAD_W:(hh + 1) * HEAD_W], qt_ref[hh],
                       preferred_element_type=F32)

    def key_tile(u):
        return jnp.where(u == 0, i, u - 1)

    def fold(hh, st, u):
        m_old = mrow[hh]
        m_new = jnp.maximum(m_old, jnp.max(st, axis=0, keepdims=True))
        pt = jnp.exp2(st - m_new).astype(BF16)
        acc[hh] = (jnp.exp2(m_old - m_new) * acc[hh]
                   + jnp.dot(vt_ref[hh, key_tile(u)], pt, preferred_element_type=F32))
        mrow[hh] = m_new

    for hh in heads:
        acc[hh] = jnp.zeros(acc.shape[1:], F32)
        mrow[hh] = jnp.full(mrow.shape[1:], MASK_VALUE, F32)
        sa[hh] = jnp.where(visible, scores(hh, i), MASK_VALUE)

    def two_tiles(u):
        for hh in heads:
            sb[hh] = scores(hh, u)
            fold(hh, sa[hh], u)
        for hh in heads:
            sa[hh] = scores(hh, u + 1)
            fold(hh, sb[hh], u + 1)

    def quad_trip(p, carry):
        two_tiles(4 * p)
        two_tiles(4 * p + 2)
        return carry

    quads = lax.shift_right_logical(i, 2)
    lax.fori_loop(0, quads, quad_trip, 0)
    u1 = 4 * quads

    def pair_trip(p, carry):
        two_tiles(u1 + 2 * p)
        return carry

    pairs = lax.shift_right_logical(i - u1, 1)
    lax.fori_loop(0, pairs, pair_trip, 0)
    u0 = u1 + 2 * pairs
    left = i + 1 - u0

    @pl.when(left == 2)
    def _():
        for hh in heads:
            sb[hh] = scores(hh, u0)
            fold(hh, sa[hh], u0)
        for hh in heads:
            fold(hh, sb[hh], u0 + 1)

    @pl.when(left == 1)
    def _():
        for hh in heads:
            fold(hh, sa[hh], u0)

    outs = []
    for hh in heads:
        a = acc[hh]
        outs.append(a[0:D_VDIM, :] / a[D_VDIM:D_VDIM + 1, :])
    o_ref[...] = jnp.concatenate(outs, axis=0).T.astype(o_ref.dtype)


def _attention(qt, k, vt, batch, seq, tq):
    t = k.shape[0]
    nq = seq // tq
    g = ATT_HEADS_PER_STEP
    return pl.pallas_call(
        functools.partial(_attn_body, tq=tq),
        grid=(batch, D_HEADS // g, nq),
        in_specs=[pl.BlockSpec((None, g, HEAD_W, tq), lambda b, h, i: (b, h, 0, i)),
                  pl.BlockSpec((seq, g * HEAD_W), lambda b, h, i: (b, h), pipeline_mode=pl.Buffered(1)),
                  pl.BlockSpec((None, g, nq, V_ROWS, tq), lambda b, h, i: (b, h, 0, 0, 0),
                               pipeline_mode=pl.Buffered(1))],
        out_specs=pl.BlockSpec((tq, g * D_VDIM), lambda b, h, i: (b * nq + i, h)),
        out_shape=jax.ShapeDtypeStruct((t, D_HEADS * D_VDIM), BF16),
        scratch_shapes=[pltpu.VMEM((g, V_ROWS, tq), F32), pltpu.VMEM((g, 1, tq), F32),
                        pltpu.VMEM((g, tq, tq), F32), pltpu.VMEM((g, tq, tq), F32)],
        compiler_params=_params("arbitrary", "arbitrary", "arbitrary"),
        name="mla_attention",
    )(qt, k, vt)


def _out_body(ya_ref, yb_ref, yc_ref, od_ref, gd_ref, x_ref, w_ref, pg_ref, o_ref):
    yd = (od_ref[...].astype(F32) * _silu(gd_ref[...])).astype(BF16)
    y = None
    for n, part in enumerate((ya_ref[...], yb_ref[...], yc_ref[...], yd)):
        t = jnp.dot(part, w_ref[n * GROUP_W:(n + 1) * GROUP_W, :], preferred_element_type=F32)
        y = t if y is None else y + t
    o_ref[...] = x_ref[...] + _rms(y, pg_ref[...])


def _outproj(ya, yb, yc, od, pd, x, w2, post_g, l, tm):
    t = x.shape[0]
    part = pl.BlockSpec((tm, GROUP_W), lambda i: (i, 0))
    return pl.pallas_call(
        _out_body,
        grid=(t // tm,),
        in_specs=[part, part, part, part, part,
                  pl.BlockSpec((tm, D_MODEL), lambda i: (i, 0)),
                  _layer_spec(w2, l), _layer_spec(post_g, l)],
        out_specs=pl.BlockSpec((tm, D_MODEL), lambda i: (i, 0)),
        out_shape=jax.ShapeDtypeStruct((t, D_MODEL), F32),
        compiler_params=_params("parallel"),
        name="outproj",
    )(ya, yb, yc, od, pd, x, w2, post_g)


def _out_in_body(ya_ref, yb_ref, yc_ref, od_ref, gd_ref, x_ref, w2_ref, pg_ref, g1_ref, w1_ref, o_ref,
                 oa, ob, oc, od, om, hscr):
    @pl.when(pl.program_id(0) == 0)
    def _():
        hscr[...] = jnp.zeros_like(hscr)

    h_prev = hscr[...]
    off = 0
    for o in (oa, ob, oc, od, om):
        n = o.shape[-1]
        o[...] = jnp.dot(h_prev, w1_ref[:, off:off + n], preferred_element_type=F32)
        off += n
    _out_body(ya_ref, yb_ref, yc_ref, od_ref, gd_ref, x_ref, w2_ref, pg_ref, o_ref)
    hscr[...] = _rms(o_ref[...], g1_ref[...]).astype(BF16)


def _outproj_inproj(ya, yb, yc, od, pd, x, w2, post_g, pre_g, w1, l, tm):
    t = x.shape[0]
    nt = t // tm
    cur = lambda i: (jnp.minimum(i, nt - 1), 0)
    prev = lambda i: (jnp.maximum(i - 1, 0), 0)
    part = pl.BlockSpec((tm, GROUP_W), cur)
    widths = (W_PA, W_PB, W_PC, W_PD, W_PM)
    res = pl.pallas_call(
        _out_in_body,
        grid=(nt + 1,),
        in_specs=[part, part, part, part, part,
                  pl.BlockSpec((tm, D_MODEL), cur),
                  _layer_spec(w2, l), _layer_spec(post_g, l),
                  _layer_spec(pre_g, l + 1), _layer_spec(w1, l + 1)],
        out_specs=[pl.BlockSpec((tm, D_MODEL), cur)] + [pl.BlockSpec((tm, n), prev) for n in widths],
        out_shape=[jax.ShapeDtypeStruct((t, D_MODEL), F32)]
                  + [jax.ShapeDtypeStruct((t, n), F32) for n in widths],
        scratch_shapes=[pltpu.VMEM((tm, D_MODEL), BF16)],
        compiler_params=_params("arbitrary"),
        name="outproj_inproj",
    )(ya, yb, yc, od, pd, x, w2, post_g, pre_g, w1)
    return res[0], res[1:]


def _rot_cols(w):
    half = D_ROPE // 2
    return jnp.concatenate([-w[..., half:], w[..., :half]], axis=-1)


def _prep_w_in(w_in):
    w_in = w_in.astype(BF16)
    d = w_in.shape[0]
    z = lambda n: jnp.zeros((d, D_MODEL, n), w_in.dtype)
    c = lambda a, n: w_in[:, :, a:a + n]
    kr = c(3204, D_ROPE)
    cols = [c(0, 768),
            c(768, 1024),
            c(1792, 768), c(2564, 256),
            c(3236, 256), c(2820, 256), c(3076, 128),
            c(2560, 4), z(60), kr, z(32),
            z(64), _rot_cols(kr), z(32)]
    return jnp.concatenate(cols, axis=-1)


def _prep_q(qb_w):
    d = qb_w.shape[0]
    z = lambda n: jnp.zeros((d, D_Q_RANK, n), qb_w.dtype)
    plain, rot = [], []
    for h in range(D_HEADS):
        o = h * (D_NOPE + D_ROPE)
        nope, rope = qb_w[:, :, o:o + D_NOPE], qb_w[:, :, o + D_NOPE:o + D_NOPE + D_ROPE]
        plain += [nope, rope, z(HEAD_W - D_NOPE - D_ROPE)]
        rot += [_rot_cols(rope)]
    t = lambda cols: jnp.swapaxes(jnp.concatenate(cols, axis=-1), 1, 2).astype(BF16)
    return t(plain), t(rot)


def _prep_kv(kvb_w):
    d = kvb_w.shape[0]
    z = lambda n: jnp.zeros((d, D_KV_RANK, n), kvb_w.dtype)
    ks, vs = [], []
    for h in range(D_HEADS):
        o = h * (D_NOPE + D_VDIM)
        ks += [kvb_w[:, :, o:o + D_NOPE], z(HEAD_W - D_NOPE)]
        vs += [kvb_w[:, :, o + D_NOPE:o + D_NOPE + D_VDIM]]
    return (jnp.concatenate(ks, axis=-1).astype(BF16),
            jnp.swapaxes(jnp.concatenate(vs, axis=-1), 1, 2).astype(BF16))


def _row(p, width=None):
    if width is not None and width > p.shape[-1]:
        p = jnp.pad(p, ((0, 0), (0, width - p.shape[-1])))
    return p[:, None, :]


def _pick_tile(n, want):
    t = min(want, n)
    while n % t:
        t //= 2
    return t


@jax.jit
def _forward(x, positions, pre_norm_g, post_norm_g, w_in, w_out, a_dw_w, a_dw_b, a_ln_g, a_ln_b,
             a_pw_w, a_pw_b, b_lb_logits, b_norm_g, c_conv_w, c_conv_b, c_dt_bias, c_a_log, c_d,
             c_norm_g, d_qa_g, d_qb_w, d_kva_g, d_kvb_w):
    batch, seq, _ = x.shape
    depth = w_in.shape[0]
    t = batch * seq
    tm = _pick_tile(seq, 512)
    tb = _pick_tile(seq, 256)

    w1 = _prep_w_in(w_in)
    w2 = w_out.astype(BF16)
    wqt, wqrt = _prep_q(d_qb_w)
    wk, wvt = _prep_kv(d_kvb_w)
    dww = jnp.pad(a_dw_w, ((0, 0), (0, 32 - A_KERNEL), (0, 0)))
    pww = a_pw_w.astype(BF16)
    dsk = _row(jnp.repeat(c_d, C_HEADDIM, axis=-1))
    alog_e = _row(jnp.repeat(c_a_log, C_HEADDIM, axis=-1))
    inv_freq = ROPE_THETA ** (-jnp.arange(0, D_ROPE, 2, dtype=F32) / D_ROPE)
    fr = jnp.concatenate([jnp.zeros((D_NOPE,), F32), inv_freq, inv_freq,
                          jnp.zeros((HEAD_W - D_NOPE - D_ROPE,), F32)])[None, :]
    frc = jnp.concatenate([inv_freq, inv_freq])[:, None]
    posc = positions.astype(F32).reshape(t, 1)
    posr = positions.astype(F32).reshape(1, t)

    rope = _rope_tables(posc, posr, fr, frc, tm)

    xf = x.reshape(t, D_MODEL)
    pa, pb, pc, pd, pm = _inproj(xf, _row(pre_norm_g), w1, 0, tm)
    for l in range(depth):
        ya = _branch_a(pa, dww, _row(a_dw_b), _row(a_ln_g), _row(a_ln_b), pww, _row(a_pw_b), l, seq, tm)
        yb = _branch_b(pb, b_lb_logits, _row(b_norm_g), l, batch, seq, tb)
        yc = _branch_c(pc, pm, c_conv_w, _row(c_conv_b), _row(c_dt_bias, LANES), alog_e,
                       dsk, _row(c_norm_g), l, batch, seq, tm)
        qt, k, vt = _d_prep(pd, pm, rope, _row(d_qa_g), wqt, wqrt, _row(d_kva_g), wk, wvt, l, batch, seq, tm)
        od = _attention(qt, k, vt, batch, seq, tm)
        if l + 1 < depth:
            xf, (pa, pb, pc, pd, pm) = _outproj_inproj(ya, yb, yc, od, pd, xf, w2, _row(post_norm_g),
                                                       _row(pre_norm_g), w1, l, tm)
        else:
            xf = _outproj(ya, yb, yc, od, pd, xf, w2, _row(post_norm_g), l, tm)
    return xf.reshape(batch, seq, D_MODEL)


def kernel(x, positions, pre_norm_g, post_norm_g, w_in, w_out, a_dw_w, a_dw_b, a_ln_g, a_ln_b, a_pw_w, a_pw_b, b_lb_logits, b_norm_g, c_conv_w, c_conv_b, c_dt_bias, c_a_log, c_d, c_norm_g, d_qa_g, d_qb_w, d_kva_g, d_kvb_w):
    return _forward(x, positions, pre_norm_g, post_norm_g, w_in, w_out, a_dw_w, a_dw_b, a_ln_g, a_ln_b,
                    a_pw_w, a_pw_b, b_lb_logits, b_norm_g, c_conv_w, c_conv_b, c_dt_bias, c_a_log, c_d,
                    c_norm_g, d_qa_g, d_qb_w, d_kva_g, d_kvb_w)
```

```python
import functools
import math

import jax
import jax.numpy as jnp
from jax import lax
from jax.experimental import pallas as pl
from jax.experimental.pallas import tpu as pltpu

F32 = jnp.float32
BF16 = jnp.bfloat16

D_MODEL = 1024
CHUNK = 64
GROUP_W = 256
NORM_EPS = 1e-6
LN_EPS = 1e-5
MASK_VALUE = -1e30
GATE_FLOOR = 1e-20
A_KERNEL = 31
B_HEADS = 4
B_DK = 64
C_HEADS = 4
C_HEADDIM = 64
C_GROUPS = 2
C_STATE = 128
C_CONV = 4
C_XBC = 768
D_HEADS = 4
D_Q_RANK = 256
D_KV_RANK = 128
D_NOPE = 64
D_ROPE = 32
D_VDIM = 64
ROPE_THETA = 10000.0

LANES = 128
SUB = 16
WIDE = 32
FACTOR_SAFE_LOG2 = 100.0
VMEM_LIMIT = 48 * 1024 * 1024

W_PA, W_PB, W_PC, W_PD, W_PM = 768, 1024, 1024, 640, 256
N_PROJ = W_PA + W_PB + W_PC + W_PD + W_PM

NT = (((1,), (1,)), ((), ()))
TN = (((0,), (0,)), ((), ()))


def _params(*sem):
    return pltpu.CompilerParams(dimension_semantics=sem, vmem_limit_bytes=VMEM_LIMIT)


def _layer_spec(arr, l):
    shape = arr.shape[1:]
    nd = len(shape)
    return pl.BlockSpec((None,) + shape, lambda *_: (l,) + (0,) * nd)


def _split3(x):
    h1 = x.astype(BF16)
    r1 = x - h1.astype(F32)
    h2 = r1.astype(BF16)
    h3 = (r1 - h2.astype(F32)).astype(BF16)
    return h1, h2, h3


def _dot_sel(sel, x, dims=None, sel_on_right=False):
    if dims is None:
        dims = (((1,), (0,)), ((), ()))
    out = None
    for piece in _split3(x):
        a, b = (piece, sel) if sel_on_right else (sel, piece)
        t = lax.dot_general(a, b, dims, preferred_element_type=F32)
        out = t if out is None else out + t
    return out


def _silu(x):
    return x * jax.nn.sigmoid(x)


def _rms(x, g):
    ms = jnp.mean(x * x, axis=-1, keepdims=True)
    return x * lax.rsqrt(ms + NORM_EPS) * g


def _inproj_body(x_ref, g_ref, w_ref, *outs):
    h = _rms(x_ref[...], g_ref[...]).astype(BF16)
    off = 0
    for o in outs:
        n = o.shape[-1]
        o[...] = jnp.dot(h, w_ref[:, off:off + n], preferred_element_type=F32).astype(o.dtype)
        off += n


def _inproj(x, pre_g, w1, l, tm):
    t = x.shape[0]
    widths = (W_PA, W_PB, W_PC, W_PD, W_PM)
    return pl.pallas_call(
        _inproj_body,
        grid=(t // tm,),
        in_specs=[pl.BlockSpec((tm, D_MODEL), lambda i: (i, 0)),
                  _layer_spec(pre_g, l), _layer_spec(w1, l)],
        out_specs=[pl.BlockSpec((tm, n), lambda i: (i, 0)) for n in widths],
        out_shape=[jax.ShapeDtypeStruct((t, n), F32) for n in widths],
        compiler_params=_params("parallel"),
        name="inproj",
    )(x, pre_g, w1)


A_HALO = 32


SUBLANES = 8


def _a_body(pa_ref, ph_ref, dww_ref, dwb_ref, lng_ref, lnb_ref, pww_ref, pwb_ref, o_ref,
            hscr, gscr, cscr, *, ts, nseq):
    first = (pl.program_id(0) % nseq) == 0
    h = pa_ref[:, 0:256] * jax.nn.sigmoid(pa_ref[:, 256:512])
    hh = ph_ref[:, 0:256] * jax.nn.sigmoid(ph_ref[:, 256:512])
    hscr[0:A_HALO, :] = jnp.where(first, 0.0, hh)
    hscr[A_HALO:A_HALO + ts, :] = h
    w = dww_ref[...]
    n8 = ts // SUBLANES + 1
    blk = SUBLANES * max(d for d in range(1, 17) if n8 % d == 0)
    for b in range(SUBLANES):
        taps = [(a, A_KERNEL - 1 - (SUBLANES * a + b)) for a in range(A_HALO // SUBLANES)
                if SUBLANES * a + b < A_KERNEL]
        for g0 in range(0, ts + SUBLANES, blk):
            acc = None
            for a, j in taps:
                s0 = g0 + A_HALO - SUBLANES - SUBLANES * a
                term = w[j:j + 1, :] * hscr[s0:s0 + blk, :]
                acc = term if acc is None else acc + term
            gscr[b, g0:g0 + blk, :] = acc
    rows = 64
    for r in range(ts // rows):
        acc = jnp.broadcast_to(dwb_ref[...], (rows, GROUP_W))
        for b in range(SUBLANES):
            s0 = SUBLANES + r * rows - b
            acc = acc + gscr[b, s0:s0 + rows, :]
        cscr[r * rows:(r + 1) * rows, :] = acc
    c = cscr[...]
    mu = jnp.mean(c, axis=-1, keepdims=True)
    d = c - mu
    var = jnp.mean(d * d, axis=-1, keepdims=True)
    y = d * lax.rsqrt(var + LN_EPS) * lng_ref[...] + lnb_ref[...]
    y = _silu(y).astype(BF16)
    y = jnp.dot(y, pww_ref[...], preferred_element_type=F32) + pwb_ref[...]
    o_ref[...] = (y * _silu(pa_ref[:, 512:768])).astype(o_ref.dtype)


def _branch_a(pa, dww, dwb, lng, lnb, pww, pwb, l, seq, ts):
    t = pa.shape[0]
    nseq = seq // ts
    hb = ts // A_HALO
    return pl.pallas_call(
        functools.partial(_a_body, ts=ts, nseq=nseq),
        grid=(t // ts,),
        in_specs=[pl.BlockSpec((ts, W_PA), lambda i: (i, 0)),
                  pl.BlockSpec((A_HALO, W_PA), lambda i: (jnp.maximum(i * hb - 1, 0), 0)),
                  _layer_spec(dww, l), _layer_spec(dwb, l), _layer_spec(lng, l),
                  _layer_spec(lnb, l), _layer_spec(pww, l), _layer_spec(pwb, l)],
        out_specs=pl.BlockSpec((ts, GROUP_W), lambda i: (i, 0)),
        out_shape=jax.ShapeDtypeStruct((t, GROUP_W), BF16),
        scratch_shapes=[pltpu.VMEM((A_HALO + ts, GROUP_W), F32),
                        pltpu.VMEM((SUBLANES, ts + SUBLANES, GROUP_W), F32),
                        pltpu.VMEM((ts, GROUP_W), F32)],
        compiler_params=_params("arbitrary"),
        name="branch_a",
    )(pa, pa, dww, dwb, lng, lnb, pww, pwb)


def _group_tril(n, group):
    r = jnp.arange(n)[:, None]
    c = jnp.arange(n)[None, :]
    return ((r // group == c // group) & (c <= r)).astype(BF16)


def _b_body(pb_ref, lbl_ref, ng_ref, tril_ref, trilsub_ref, o_ref, st_ref, kscr, bscr, oscr, *, l, tb):
    @pl.when(pl.program_id(1) == 0)
    def _():
        st_ref[...] = jnp.zeros_like(st_ref)

    lg = lbl_ref[...]
    e = jnp.exp(lg - jnp.max(lg, axis=0, keepdims=True))
    p = e / jnp.sum(e, axis=0, keepdims=True)
    lb = jnp.zeros((1, GROUP_W), F32)
    for m in range(1, l + 1):
        lb = lb + p[m:m + 1, :]

    zf = pb_ref[:, 256:512]
    fg = lb + (1.0 - lb) * jax.nn.sigmoid(zf)
    logf = jnp.log(jnp.maximum(fg, GATE_FLOOR))
    kscr[...] = (1.0 - lb) * jax.nn.sigmoid(-zf)
    bscr[...] = _dot_sel(tril_ref[...], logf) * LOG2E

    hr = lax.broadcasted_iota(jnp.int32, (GROUP_W, GROUP_W), 0) // B_DK
    hc = lax.broadcasted_iota(jnp.int32, (GROUP_W, GROUP_W), 1) // B_DK
    same_head = hr == hc
    bd = jnp.where(same_head, 1.0, 0.0).astype(BF16)
    half = SUB // 2
    tio = lax.broadcasted_iota(jnp.int32, (SUB, GROUP_W), 0)
    tio_hi = lax.broadcasted_iota(jnp.int32, (half, GROUP_W), 0) + half
    zero_lo = jnp.zeros((half, GROUP_W), F32)

    group = 8
    gr = group * SUB
    lane_head = lax.broadcasted_iota(jnp.int32, (gr, GROUP_W), 1) // B_DK

    def stack_heads(x):
        return jnp.concatenate([jnp.where(lane_head == h, x, 0.0) for h in range(B_HEADS)], axis=0).astype(BF16)

    prow = lax.broadcasted_iota(jnp.int32, (gr, B_HEADS * gr), 0)
    pcol = lax.broadcasted_iota(jnp.int32, (gr, B_HEADS * gr), 1) % gr
    same_block_causal = (prow // WIDE == pcol // WIDE) & (prow >= pcol)

    def intra_factored(qg, kg, vg, bg):
        qt = (qg * jnp.exp2(bg)).astype(BF16)
        p = lax.dot_general(qt, stack_heads(kg * jnp.exp2(-bg)), NT, preferred_element_type=F32)
        p = jnp.where(same_block_causal, p, 0.0).astype(BF16)
        return jnp.dot(p, stack_heads(vg), preferred_element_type=F32)

    def intra_pairwise_group(qg, kg, vg, bg):
        parts = [slice(n * SUB, (n + 1) * SUB) for n in range(group)]
        return jnp.concatenate([intra_pairwise(qg[p], kg[p], vg[p], bg[p]) for p in parts], axis=0)

    def intra_pairwise(qj, kj, vj, bj):
        ws = []
        for s in range(SUB):
            if s < half:
                w = jnp.where(tio >= s, qj * kj[s:s + 1, :] * jnp.exp2(bj - bj[s:s + 1, :]), 0.0)
            else:
                w = jnp.where(tio_hi >= s, qj[half:] * kj[s:s + 1, :] * jnp.exp2(bj[half:] - bj[s:s + 1, :]), 0.0)
                w = jnp.concatenate([zero_lo, w], axis=0)
            ws.append(w.astype(BF16))
        m = jnp.dot(jnp.concatenate(ws, axis=0), bd, preferred_element_type=F32)
        o_lo = jnp.zeros((half, GROUP_W), F32)
        o_hi = jnp.zeros((half, GROUP_W), F32)
        for s in range(SUB):
            vrow = vj[s:s + 1, :]
            if s < half:
                o_lo = o_lo + m[s * SUB:s * SUB + half, :] * vrow
            o_hi = o_hi + m[s * SUB + half:(s + 1) * SUB, :] * vrow
        return jnp.concatenate([o_lo, o_hi], axis=0)

    def state_step(st, qj, kj, vj, bj):
        qt = (qj * jnp.exp2(bj)).astype(BF16)
        o = lax.dot_general(qt, st.astype(BF16), NT, preferred_element_type=F32)
        nr = bj.shape[0]
        blast = bj[nr - 1:nr, :]
        kh = (kj * jnp.exp2(blast - bj)).astype(BF16)
        u = lax.dot_general(vj.astype(BF16), kh, TN, preferred_element_type=F32)
        return o, st * jnp.exp2(blast) + jnp.where(same_head, u, 0.0)

    def run(intra_group, b_ref, block):
        def trip(jj, carry):
            r0 = pl.multiple_of(jj * gr, gr)
            rows = pl.ds(r0, gr)
            qg, vg = pb_ref[rows, 0:256], pb_ref[rows, 512:768]
            kg, bg = kscr[rows, :], b_ref[rows, :]
            st = st_ref[...]
            o_intra = intra_group(qg, kg, vg, bg)
            outs = []
            for n in range(gr // block):
                p = slice(n * block, (n + 1) * block)
                o_inter, st = state_step(st, qg[p], kg[p], vg[p], bg[p])
                outs.append(o_inter)
            oscr[rows, :] = o_intra + jnp.concatenate(outs, axis=0)
            st_ref[...] = st
            return carry

        lax.fori_loop(0, tb // gr, trip, 0)

    mild = jnp.min(bscr[...]) > -FACTOR_SAFE_LOG2

    @pl.when(mild)
    def _():
        run(intra_factored, bscr, WIDE)

    @pl.when(jnp.logical_not(mild))
    def _():
        bscr[...] = _dot_sel(trilsub_ref[...], logf) * LOG2E
        run(intra_pairwise_group, bscr, SUB)

    o = oscr[...]
    ms = _dot_sel(bd, o * o, sel_on_right=True) * (1.0 / B_DK)
    y = o * lax.rsqrt(ms + NORM_EPS) * ng_ref[...]
    o_ref[...] = (y * _silu(pb_ref[:, 768:1024])).astype(o_ref.dtype)


def _branch_b(pb, lbl, ng, l, batch, seq, tb):
    t = pb.shape[0]
    nseq = seq // tb
    return pl.pallas_call(
        functools.partial(_b_body, l=l, tb=tb),
        grid=(batch, nseq),
        in_specs=[pl.BlockSpec((tb, W_PB), lambda b, j: (b * nseq + j, 0)),
                  pl.BlockSpec(lbl.shape, lambda b, j: (0, 0)),
                  _layer_spec(ng, l),
                  pl.BlockSpec((tb, tb), lambda b, j: (0, 0)),
                  pl.BlockSpec((tb, tb), lambda b, j: (0, 0))],
        out_specs=pl.BlockSpec((tb, GROUP_W), lambda b, j: (b * nseq + j, 0)),
        out_shape=jax.ShapeDtypeStruct((t, GROUP_W), BF16),
        scratch_shapes=[pltpu.VMEM((GROUP_W, GROUP_W), F32),
                        pltpu.VMEM((tb, GROUP_W), F32),
                        pltpu.VMEM((tb, GROUP_W), F32),
                        pltpu.VMEM((tb, GROUP_W), F32)],
        compiler_params=_params("arbitrary", "arbitrary"),
        name="branch_b",
    )(pb, lbl, ng, _group_tril(tb, WIDE), _group_tril(tb, SUB))


C_HALO = 8


def _c_body(pc_ref, ph_ref, pm_ref, cw_ref, cb_ref, dtb_ref, alog_ref, dsk_ref, ng_ref, tril_ref, same_ref,
            exp_ref, o_ref, h_ref, xscr, cscr, aescr, lmscr, xdscr, yscr, *, tc):
    first = pl.program_id(1) == 0

    @pl.when(first)
    def _():
        h_ref[...] = jnp.zeros_like(h_ref)

    xscr[0:C_HALO, :] = jnp.where(first, 0.0, ph_ref[:, 0:C_XBC])
    xscr[C_HALO:C_HALO + tc, :] = pc_ref[:, 0:C_XBC]
    rows = 64
    base = C_HALO - (C_CONV - 1)
    for lb in range(C_XBC // GROUP_W):
        ls = slice(lb * GROUP_W, (lb + 1) * GROUP_W)
        w = cw_ref[:, ls]
        bias = jnp.broadcast_to(cb_ref[:, ls], (rows, GROUP_W))
        for r in range(tc // rows):
            acc = bias
            for j in range(C_CONV):
                s0 = r * rows + base + j
                acc = acc + w[j:j + 1, :] * xscr[s0:s0 + rows, ls]
            cscr[r * rows:(r + 1) * rows, ls] = _silu(acc)

    dt = jnp.logaddexp(pm_ref[:, 0:LANES] + dtb_ref[...], 0.0)
    dte = _dot_sel(exp_ref[...], dt, sel_on_right=True)
    xdscr[...] = cscr[:, 0:GROUP_W] * dte
    a = dte * (-jnp.exp(alog_ref[...]))
    tb = tril_ref.shape[0]
    tpos = lax.broadcasted_iota(jnp.int32, (tb, GROUP_W), 0) % CHUNK
    spos = lax.broadcasted_iota(jnp.int32, (tb, GROUP_W), 1) % C_HEADDIM
    for r in range(tc // tb):
        rows = slice(r * tb, (r + 1) * tb)
        acs = _dot_sel(tril_ref[...], a[rows, :])
        aescr[rows, :] = acs
        acs_t = _dot_sel(same_ref[...], jnp.where(tpos == spos, acs, 0.0))
        lmscr[rows, :] = jnp.where(tpos >= spos, jnp.exp(jnp.minimum(acs - acs_t, 0.0)), 0.0)

    e = C_HEADS // C_GROUPS
    gw = e * C_HEADDIM
    first_head = lax.broadcasted_iota(jnp.int32, (CHUNK, gw), 1) < C_HEADDIM
    dsk = dsk_ref[...]

    def chunk(c, carry):
        r0 = pl.multiple_of(c * CHUNK, CHUNK)
        rs = pl.ds(r0, CHUNK)
        ae = aescr[rs, :]
        alast = ae[CHUNK - 1:CHUNK, :]
        xs = cscr[rs, 0:GROUP_W]
        xdt = xdscr[rs, :]
        wx = xdt * jnp.exp(alast - ae)
        for g in range(C_GROUPS):
            ls = slice(g * gw, (g + 1) * gw)
            bmg = cscr[rs, GROUP_W + g * C_STATE:GROUP_W + (g + 1) * C_STATE].astype(BF16)
            c0 = GROUP_W + C_GROUPS * C_STATE + g * C_STATE
            cmg = cscr[rs, c0:c0 + C_STATE].astype(BF16)
            cb2 = lax.dot_general(cmg, jnp.concatenate([bmg, bmg], axis=0), NT, preferred_element_type=F32)
            xg = xdt[:, ls]
            x2 = jnp.concatenate([jnp.where(first_head, xg, 0.0), jnp.where(first_head, 0.0, xg)], axis=0)
            ydiag = jnp.dot((cb2 * lmscr[rs, ls]).astype(BF16), x2.astype(BF16), preferred_element_type=F32)
            ht = h_ref[g * C_STATE:(g + 1) * C_STATE, :]
            yoff = jnp.dot(cmg, ht.astype(BF16), preferred_element_type=F32) * jnp.exp(ae[:, ls])
            stt = lax.dot_general(bmg, wx[:, ls].astype(BF16), TN, preferred_element_type=F32)
            h_ref[g * C_STATE:(g + 1) * C_STATE, :] = ht * jnp.exp(alast[:, ls]) + stt
            yscr[rs, ls] = ydiag + yoff + xs[:, ls] * dsk[:, ls]
        return carry

    lax.fori_loop(0, tc // CHUNK, chunk, 0, unroll=8)

    yz = yscr[...] * _silu(pc_ref[:, C_XBC:C_XBC + GROUP_W])
    gw = e * C_HEADDIM
    for g in range(C_GROUPS):
        o_ref[:, g * gw:(g + 1) * gw] = _rms(yz[:, g * gw:(g + 1) * gw],
                                             ng_ref[:, g * gw:(g + 1) * gw]).astype(o_ref.dtype)


def _branch_c(pc, pm, cw, cb, dtb, alog, dsk, ng, l, batch, seq, tc):
    t = pc.shape[0]
    nseq = seq // tc
    hb = tc // C_HALO
    tt = min(tc, 256)
    return pl.pallas_call(
        functools.partial(_c_body, tc=tc),
        grid=(batch, nseq),
        in_specs=[pl.BlockSpec((tc, W_PC), lambda b, j: (b * nseq + j, 0)),
                  pl.BlockSpec((C_HALO, W_PC), lambda b, j: (jnp.maximum((b * nseq + j) * hb - 1, 0), 0)),
                  pl.BlockSpec((tc, W_PM), lambda b, j: (b * nseq + j, 0)),
                  _layer_spec(cw, l), _layer_spec(cb, l), _layer_spec(dtb, l),
                  _layer_spec(alog, l), _layer_spec(dsk, l), _layer_spec(ng, l),
                  pl.BlockSpec((tt, tt), lambda b, j: (0, 0)),
                  pl.BlockSpec((tt, tt), lambda b, j: (0, 0)),
                  pl.BlockSpec((LANES, GROUP_W), lambda b, j: (0, 0))],
        out_specs=pl.BlockSpec((tc, GROUP_W), lambda b, j: (b * nseq + j, 0)),
        out_shape=jax.ShapeDtypeStruct((t, GROUP_W), BF16),
        scratch_shapes=[pltpu.VMEM((C_GROUPS * C_STATE, C_HEADS // C_GROUPS * C_HEADDIM), F32),
                        pltpu.VMEM((C_HALO + tc, C_XBC), F32),
                        pltpu.VMEM((tc, C_XBC), F32),
                        pltpu.VMEM((tc, GROUP_W), F32),
                        pltpu.VMEM((tc, GROUP_W), F32),
                        pltpu.VMEM((tc, GROUP_W), F32),
                        pltpu.VMEM((tc, GROUP_W), F32)],
        compiler_params=_params("arbitrary", "arbitrary"),
        name="branch_c",
    )(pc, pc, pm, cw, cb, dtb, alog, dsk, ng, _group_tril(tt, CHUNK), _group_ones(tt, CHUNK), _head_expand())


def _group_ones(n, group):
    r = jnp.arange(n)[:, None] // group
    c = jnp.arange(n)[None, :] // group
    return (r == c).astype(BF16)


def _head_expand():
    r = jnp.arange(LANES)[:, None]
    c = jnp.arange(GROUP_W)[None, :]
    return (r == c // C_HEADDIM).astype(BF16)


HEAD_W = 128
V_ROWS = 80
QK_SCALE = (D_NOPE + D_ROPE) ** -0.5
LOG2E = 1.4426950408889634


def _rope_body(posc_ref, posr_ref, fr_ref, frc_ref, cs_ref, sn_ref, cst_ref, snt_ref):
    ang = posc_ref[...] * fr_ref[...]
    cs_ref[...] = jnp.cos(ang)
    sn_ref[...] = jnp.sin(ang)
    angt = frc_ref[...] * posr_ref[...]
    cst_ref[...] = jnp.cos(angt)
    snt_ref[...] = jnp.sin(angt)


def _rope_tables(posc, posr, fr, frc, tm):
    t = posc.shape[0]
    rows = pl.BlockSpec((tm, LANES), lambda i: (i, 0))
    cols = pl.BlockSpec((D_ROPE, tm), lambda i: (0, i))
    return pl.pallas_call(
        _rope_body,
        grid=(t // tm,),
        in_specs=[pl.BlockSpec((tm, 1), lambda i: (i, 0)), pl.BlockSpec((1, tm), lambda i: (0, i)),
                  pl.BlockSpec(fr.shape, lambda i: (0, 0)), pl.BlockSpec(frc.shape, lambda i: (0, 0))],
        out_specs=[rows, rows, cols, cols],
        out_shape=[jax.ShapeDtypeStruct((t, LANES), F32)] * 2 + [jax.ShapeDtypeStruct((D_ROPE, t), F32)] * 2,
        compiler_params=_params("parallel"),
        name="rope_tables",
    )(posc, posr, fr, frc)


def _d_prep_body(pd_ref, pm_ref, cs_ref, sn_ref, cst_ref, snt_ref, qag_ref, wqt_ref, wqrt_ref, kvag_ref,
                 wk_ref, wvt_ref, qt_ref, k_ref, vt_ref):
    sc = QK_SCALE * LOG2E
    hq = _rms(pd_ref[:, 256:512], qag_ref[...]).astype(BF16)
    q0 = lax.dot_general(wqt_ref[...], hq, NT, preferred_element_type=F32)
    q1 = lax.dot_general(wqrt_ref[...], hq, NT, preferred_element_type=F32)
    cst = cst_ref[...]
    snt = snt_ref[...]
    for h in range(D_HEADS):
        r0 = h * HEAD_W
        qt_ref[h, 0:D_NOPE, :] = (q0[r0:r0 + D_NOPE] * sc).astype(BF16)
        rope = q0[r0 + D_NOPE:r0 + D_NOPE + D_ROPE] * cst + q1[h * D_ROPE:(h + 1) * D_ROPE] * snt
        qt_ref[h, D_NOPE:D_NOPE + D_ROPE, :] = (rope * sc).astype(BF16)
        qt_ref[h, D_NOPE + D_ROPE:HEAD_W, :] = jnp.zeros((HEAD_W - D_NOPE - D_ROPE, q0.shape[1]), BF16)

    hkv = _rms(pd_ref[:, 512:640], kvag_ref[...]).astype(BF16)
    k0 = jnp.dot(hkv, wk_ref[...], preferred_element_type=F32)
    cs = cs_ref[...]
    sn = sn_ref[...]
    lane = lax.broadcasted_iota(jnp.int32, cs.shape, 1)
    rope_lane = (lane >= D_NOPE) & (lane < D_NOPE + D_ROPE)
    kr = jnp.where(rope_lane, pm_ref[:, 0:LANES] * cs, 0.0) + pm_ref[:, LANES:2 * LANES] * sn
    k_ref[...] = (k0 + jnp.concatenate([kr] * D_HEADS, axis=1)).astype(BF16)

    vt = lax.dot_general(wvt_ref[...], hkv, NT, preferred_element_type=F32)
    pad = V_ROWS - D_VDIM
    ones_row = jnp.where(lax.broadcasted_iota(jnp.int32, (pad, vt.shape[1]), 0) == 0, 1.0, 0.0).astype(BF16)
    for h in range(D_HEADS):
        vt_ref[h, 0:D_VDIM, :] = vt[h * D_VDIM:(h + 1) * D_VDIM].astype(BF16)
        vt_ref[h, D_VDIM:V_ROWS, :] = ones_row


def _d_prep(pd, pm, rope, qag, wqt, wqrt, kvag, wk, wvt, l, batch, seq, tm):
    t = pd.shape[0]
    nseq = seq // tm
    cs, sn, cst, snt = rope
    rows = pl.BlockSpec((tm, LANES), lambda b, i: (b * nseq + i, 0))
    cols = pl.BlockSpec((D_ROPE, tm), lambda b, i: (0, b * nseq + i))
    return pl.pallas_call(
        _d_prep_body,
        grid=(batch, nseq),
        in_specs=[pl.BlockSpec((tm, W_PD), lambda b, i: (b * nseq + i, 0)),
                  pl.BlockSpec((tm, W_PM), lambda b, i: (b * nseq + i, 0)),
                  rows, rows, cols, cols,
                  _layer_spec(qag, l), _layer_spec(wqt, l), _layer_spec(wqrt, l),
                  _layer_spec(kvag, l), _layer_spec(wk, l), _layer_spec(wvt, l)],
        out_specs=[pl.BlockSpec((None, D_HEADS, HEAD_W, tm), lambda b, i: (b, 0, 0, i)),
                   pl.BlockSpec((tm, D_HEADS * HEAD_W), lambda b, i: (b * nseq + i, 0)),
                   pl.BlockSpec((None, D_HEADS, None, V_ROWS, tm), lambda b, i: (b, 0, i, 0, 0))],
        out_shape=[jax.ShapeDtypeStruct((batch, D_HEADS, HEAD_W, seq), BF16),
                   jax.ShapeDtypeStruct((t, D_HEADS * HEAD_W), BF16),
                   jax.ShapeDtypeStruct((batch, D_HEADS, nseq, V_ROWS, tm), BF16)],
        compiler_params=_params("parallel", "parallel"),
        name="mla_prep",
    )(pd, pm, cs, sn, cst, snt, qag, wqt, wqrt, kvag, wk, wvt)


ATT_HEADS_PER_STEP = 4


def _attn_body(qt_ref, k_ref, vt_ref, o_ref, acc, mrow, sa, sb, *, tq):
    i = pl.program_id(2)
    key_chunk = lax.broadcasted_iota(jnp.int32, (tq, tq), 0) // CHUNK
    qry_chunk = lax.broadcasted_iota(jnp.int32, (tq, tq), 1) // CHUNK
    visible = key_chunk <= qry_chunk
    heads = range(ATT_HEADS_PER_STEP)

    def scores(hh, kt):
        k0 = pl.multiple_of(kt * tq, tq)
        return jnp.dot(k_ref[pl.ds(k0, tq), hh * HEAD_W:(hh + 1) * HEAD_W], qt_ref[hh],
                       preferred_element_type=F32)

    def key_tile(u):
        return jnp.where(u == 0, i, u - 1)

    def fold(hh, st, u):
        m_old = mrow[hh]
        m_new = jnp.maximum(m_old, jnp.max(st, axis=0, keepdims=True))
        pt = jnp.exp2(st - m_new).astype(BF16)
        acc[hh] = (jnp.exp2(m_old - m_new) * acc[hh]
                   + jnp.dot(vt_ref[hh, key_tile(u)], pt, preferred_element_type=F32))
        mrow[hh] = m_new

    for hh in heads:
        acc[hh] = jnp.zeros(acc.shape[1:], F32)
        mrow[hh] = jnp.full(mrow.shape[1:], MASK_VALUE, F32)
        sa[hh] = jnp.where(visible, scores(hh, i), MASK_VALUE)

    def two_tiles(u):
        for hh in heads:
            sb[hh] = scores(hh, u)
            fold(hh, sa[hh], u)
        for hh in heads:
            sa[hh] = scores(hh, u + 1)
            fold(hh, sb[hh], u + 1)

    def quad_trip(p, carry):
        two_tiles(4 * p)
        two_tiles(4 * p + 2)
        return carry

    quads = lax.shift_right_logical(i, 2)
    lax.fori_loop(0, quads, quad_trip, 0)
    u1 = 4 * quads

    def pair_trip(p, carry):
        two_tiles(u1 + 2 * p)
        return carry

    pairs = lax.shift_right_logical(i - u1, 1)
    lax.fori_loop(0, pairs, pair_trip, 0)
    u0 = u1 + 2 * pairs
    left = i + 1 - u0

    @pl.when(left == 2)
    def _():
        for hh in heads:
            sb[hh] = scores(hh, u0)
            fold(hh, sa[hh], u0)
        for hh in heads:
            fold(hh, sb[hh], u0 + 1)

    @pl.when(left == 1)
    def _():
        for hh in heads:
            fold(hh, sa[hh], u0)

    outs = []
    for hh in heads:
        a = acc[hh]
        outs.append(a[0:D_VDIM, :] / a[D_VDIM:D_VDIM + 1, :])
    o_ref[...] = jnp.concatenate(outs, axis=0).T.astype(o_ref.dtype)


def _attention(qt, k, vt, batch, seq, tq):
    t = k.shape[0]
    nq = seq // tq
    g = ATT_HEADS_PER_STEP
    return pl.pallas_call(
        functools.partial(_attn_body, tq=tq),
        grid=(batch, D_HEADS // g, nq),
        in_specs=[pl.BlockSpec((None, g, HEAD_W, tq), lambda b, h, i: (b, h, 0, i)),
                  pl.BlockSpec((seq, g * HEAD_W), lambda b, h, i: (b, h), pipeline_mode=pl.Buffered(1)),
                  pl.BlockSpec((None, g, nq, V_ROWS, tq), lambda b, h, i: (b, h, 0, 0, 0),
                               pipeline_mode=pl.Buffered(1))],
        out_specs=pl.BlockSpec((tq, g * D_VDIM), lambda b, h, i: (b * nq + i, h)),
        out_shape=jax.ShapeDtypeStruct((t, D_HEADS * D_VDIM), BF16),
        scratch_shapes=[pltpu.VMEM((g, V_ROWS, tq), F32), pltpu.VMEM((g, 1, tq), F32),
                        pltpu.VMEM((g, tq, tq), F32), pltpu.VMEM((g, tq, tq), F32)],
        compiler_params=_params("arbitrary", "arbitrary", "arbitrary"),
        name="mla_attention",
    )(qt, k, vt)


def _out_body(ya_ref, yb_ref, yc_ref, od_ref, gd_ref, x_ref, w_ref, pg_ref, o_ref):
    yd = (od_ref[...].astype(F32) * _silu(gd_ref[...])).astype(BF16)
    y = None
    for n, part in enumerate((ya_ref[...], yb_ref[...], yc_ref[...], yd)):
        t = jnp.dot(part, w_ref[n * GROUP_W:(n + 1) * GROUP_W, :], preferred_element_type=F32)
        y = t if y is None else y + t
    o_ref[...] = x_ref[...] + _rms(y, pg_ref[...])


def _outproj(ya, yb, yc, od, pd, x, w2, post_g, l, tm):
    t = x.shape[0]
    part = pl.BlockSpec((tm, GROUP_W), lambda i: (i, 0))
    return pl.pallas_call(
        _out_body,
        grid=(t // tm,),
        in_specs=[part, part, part, part, part,
                  pl.BlockSpec((tm, D_MODEL), lambda i: (i, 0)),
                  _layer_spec(w2, l), _layer_spec(post_g, l)],
        out_specs=pl.BlockSpec((tm, D_MODEL), lambda i: (i, 0)),
        out_shape=jax.ShapeDtypeStruct((t, D_MODEL), F32),
        compiler_params=_params("parallel"),
        name="outproj",
    )(ya, yb, yc, od, pd, x, w2, post_g)


def _out_in_body(ya_ref, yb_ref, yc_ref, od_ref, gd_ref, x_ref, w2_ref, pg_ref, g1_ref, w1_ref, o_ref,
                 oa, ob, oc, od, om, hscr):
    @pl.when(pl.program_id(0) == 0)
    def _():
        hscr[...] = jnp.zeros_like(hscr)

    h_prev = hscr[...]
    off = 0
    for o in (oa, ob, oc, od, om):
        n = o.shape[-1]
        o[...] = jnp.dot(h_prev, w1_ref[:, off:off + n], preferred_element_type=F32)
        off += n
    _out_body(ya_ref, yb_ref, yc_ref, od_ref, gd_ref, x_ref, w2_ref, pg_ref, o_ref)
    hscr[...] = _rms(o_ref[...], g1_ref[...]).astype(BF16)


def _outproj_inproj(ya, yb, yc, od, pd, x, w2, post_g, pre_g, w1, l, tm):
    t = x.shape[0]
    nt = t // tm
    cur = lambda i: (jnp.minimum(i, nt - 1), 0)
    prev = lambda i: (jnp.maximum(i - 1, 0), 0)
    part = pl.BlockSpec((tm, GROUP_W), cur)
    widths = (W_PA, W_PB, W_PC, W_PD, W_PM)
    res = pl.pallas_call(
        _out_in_body,
        grid=(nt + 1,),
        in_specs=[part, part, part, part, part,
                  pl.BlockSpec((tm, D_MODEL), cur),
                  _layer_spec(w2, l), _layer_spec(post_g, l),
                  _layer_spec(pre_g, l + 1), _layer_spec(w1, l + 1)],
        out_specs=[pl.BlockSpec((tm, D_MODEL), cur)] + [pl.BlockSpec((tm, n), prev) for n in widths],
        out_shape=[jax.ShapeDtypeStruct((t, D_MODEL), F32)]
                  + [jax.ShapeDtypeStruct((t, n), F32) for n in widths],
        scratch_shapes=[pltpu.VMEM((tm, D_MODEL), BF16)],
        compiler_params=_params("arbitrary"),
        name="outproj_inproj",
    )(ya, yb, yc, od, pd, x, w2, post_g, pre_g, w1)
    return res[0], res[1:]


def _rot_cols(w):
    half = D_ROPE // 2
    return jnp.concatenate([-w[..., half:], w[..., :half]], axis=-1)


def _prep_w_in(w_in):
    w_in = w_in.astype(BF16)
    d = w_in.shape[0]
    z = lambda n: jnp.zeros((d, D_MODEL, n), w_in.dtype)
    c = lambda a, n: w_in[:, :, a:a + n]
    kr = c(3204, D_ROPE)
    cols = [c(0, 768),
            c(768, 1024),
            c(1792, 768), c(2564, 256),
            c(3236, 256), c(2820, 256), c(3076, 128),
            c(2560, 4), z(60), kr, z(32),
            z(64), _rot_cols(kr), z(32)]
    return jnp.concatenate(cols, axis=-1)


def _prep_q(qb_w):
    d = qb_w.shape[0]
    z = lambda n: jnp.zeros((d, D_Q_RANK, n), qb_w.dtype)
    plain, rot = [], []
    for h in range(D_HEADS):
        o = h * (D_NOPE + D_ROPE)
        nope, rope = qb_w[:, :, o:o + D_NOPE], qb_w[:, :, o + D_NOPE:o + D_NOPE + D_ROPE]
        plain += [nope, rope, z(HEAD_W - D_NOPE - D_ROPE)]
        rot += [_rot_cols(rope)]
    t = lambda cols: jnp.swapaxes(jnp.concatenate(cols, axis=-1), 1, 2).astype(BF16)
    return t(plain), t(rot)


def _prep_kv(kvb_w):
    d = kvb_w.shape[0]
    z = lambda n: jnp.zeros((d, D_KV_RANK, n), kvb_w.dtype)
    ks, vs = [], []
    for h in range(D_HEADS):
        o = h * (D_NOPE + D_VDIM)
        ks += [kvb_w[:, :, o:o + D_NOPE], z(HEAD_W - D_NOPE)]
        vs += [kvb_w[:, :, o + D_NOPE:o + D_NOPE + D_VDIM]]
    return (jnp.concatenate(ks, axis=-1).astype(BF16),
            jnp.swapaxes(jnp.concatenate(vs, axis=-1), 1, 2).astype(BF16))


def _row(p, width=None):
    if width is not None and width > p.shape[-1]:
        p = jnp.pad(p, ((0, 0), (0, width - p.shape[-1])))
    return p[:, None, :]


def _pick_tile(n, want):
    t = min(want, n)
    while n % t:
        t //= 2
    return t


@jax.jit
def _forward(x, positions, pre_norm_g, post_norm_g, w_in, w_out, a_dw_w, a_dw_b, a_ln_g, a_ln_b,
             a_pw_w, a_pw_b, b_lb_logits, b_norm_g, c_conv_w, c_conv_b, c_dt_bias, c_a_log, c_d,
             c_norm_g, d_qa_g, d_qb_w, d_kva_g, d_kvb_w):
    batch, seq, _ = x.shape
    depth = w_in.shape[0]
    t = batch * seq
    tm = _pick_tile(seq, 512)
    tb = _pick_tile(seq, 256)

    w1 = _prep_w_in(w_in)
    w2 = w_out.astype(BF16)
    wqt, wqrt = _prep_q(d_qb_w)
    wk, wvt = _prep_kv(d_kvb_w)
    dww = jnp.pad(a_dw_w, ((0, 0), (0, 32 - A_KERNEL), (0, 0)))
    pww = a_pw_w.astype(BF16)
    dsk = _row(jnp.repeat(c_d, C_HEADDIM, axis=-1))
    alog_e = _row(jnp.repeat(c_a_log, C_HEADDIM, axis=-1))
    inv_freq = ROPE_THETA ** (-jnp.arange(0, D_ROPE, 2, dtype=F32) / D_ROPE)
    fr = jnp.concatenate([jnp.zeros((D_NOPE,), F32), inv_freq, inv_freq,
                          jnp.zeros((HEAD_W - D_NOPE - D_ROPE,), F32)])[None, :]
    frc = jnp.concatenate([inv_freq, inv_freq])[:, None]
    posc = positions.astype(F32).reshape(t, 1)
    posr = positions.astype(F32).reshape(1, t)

    rope = _rope_tables(posc, posr, fr, frc, tm)

    xf = x.reshape(t, D_MODEL)
    pa, pb, pc, pd, pm = _inproj(xf, _row(pre_norm_g), w1, 0, tm)
    for l in range(depth):
        ya = _branch_a(pa, dww, _row(a_dw_b), _row(a_ln_g), _row(a_ln_b), pww, _row(a_pw_b), l, seq, tm)
        yb = _branch_b(pb, b_lb_logits, _row(b_norm_g), l, batch, seq, tb)
        yc = _branch_c(pc, pm, c_conv_w, _row(c_conv_b), _row(c_dt_bias, LANES), alog_e,
                       dsk, _row(c_norm_g), l, batch, seq, tm)
        qt, k, vt = _d_prep(pd, pm, rope, _row(d_qa_g), wqt, wqrt, _row(d_kva_g), wk, wvt, l, batch, seq, tm)
        od = _attention(qt, k, vt, batch, seq, tm)
        if l + 1 < depth:
            xf, (pa, pb, pc, pd, pm) = _outproj_inproj(ya, yb, yc, od, pd, xf, w2, _row(post_norm_g),
                                                       _row(pre_norm_g), w1, l, tm)
        else:
            xf = _outproj(ya, yb, yc, od, pd, xf, w2, _row(post_norm_g), l, tm)
    return xf.reshape(batch, seq, D_MODEL)


def kernel(x, positions, pre_norm_g, post_norm_g, w_in, w_out, a_dw_w, a_dw_b, a_ln_g, a_ln_b, a_pw_w, a_pw_b, b_lb_logits, b_norm_g, c_conv_w, c_conv_b, c_dt_bias, c_a_log, c_d, c_norm_g, d_qa_g, d_qb_w, d_kva_g, d_kvb_w):
    return _forward(x, positions, pre_norm_g, post_norm_g, w_in, w_out, a_dw_w, a_dw_b, a_ln_g, a_ln_b,
                    a_pw_w, a_pw_b, b_lb_logits, b_norm_g, c_conv_w, c_conv_b, c_dt_bias, c_a_log, c_d,
                    c_norm_g, d_qa_g, d_qb_w, d_kva_g, d_kvb_w)
```
